```python
import jax, jax.numpy as jnp
from jax import lax
import numpy as np

D_MODEL = 1024
BATCH = 32
SEQ = 2048
DEPTH = 1

D_FF = 2816
GLA_HEADS = 4
GLA_DK = D_MODEL // 2
GLA_DV = D_MODEL
GLA_HK = GLA_DK // GLA_HEADS
GLA_HV = GLA_DV // GLA_HEADS
GLA_GATE_RANK = 16
GLA_GATE_NORMALIZER = 16.0
GLA_CHUNK = 64
CONV_DIM = D_MODEL
CONV_WIDTH = 31
SPLIT_SIZES = (GLA_DK, GLA_DK, GLA_DV, GLA_GATE_RANK, GLA_DV, 2 * CONV_DIM, 2 * D_MODEL)
SPLIT_POINTS = tuple(int(s) for s in np.cumsum(SPLIT_SIZES)[:-1])
MIX_IN_WIDTH = int(sum(SPLIT_SIZES))
EPS = 1e-6

kernel_name = "hybrid_gla_conformer_conv_macaron"


def rms_norm(x, g):
    x32 = x.astype(jnp.float32)
    y = x32 * lax.rsqrt(jnp.mean(x32 * x32, axis=-1, keepdims=True) + EPS)
    return y.astype(x.dtype) * g


def layer_norm(x, g, b):
    x32 = x.astype(jnp.float32)
    mu = jnp.mean(x32, axis=-1, keepdims=True)
    var = jnp.mean(jnp.square(x32 - mu), axis=-1, keepdims=True)
    return ((x32 - mu) * lax.rsqrt(var + EPS)).astype(x.dtype) * g + b


def swiglu_ffn(h, w_in, w_out):
    gate, up = jnp.split(h @ w_in, 2, axis=-1)
    return (jax.nn.silu(gate) * up) @ w_out


def gla_chunked(q, k, v, log_a):
    B, S, H, HK = q.shape
    HV = v.shape[-1]
    n = S // GLA_CHUNK

    def to_chunks(t):
        return t.reshape(B, n, GLA_CHUNK, H, t.shape[-1]).transpose(1, 0, 3, 2, 4)

    qc, kc, vc = to_chunks(q), to_chunks(k), to_chunks(v)
    bc = jnp.cumsum(to_chunks(log_a), axis=-2)
    causal = jnp.tril(jnp.ones((GLA_CHUNK, GLA_CHUNK), dtype=bool))[:, :, None]

    def step(state, inp):
        q_, k_, v_, b_ = inp
        diff = b_[..., :, None, :] - b_[..., None, :, :]
        decay = jnp.exp(jnp.where(causal, diff, -jnp.inf))
        attn = jnp.einsum('bhik,bhjk,bhijk->bhij', q_, k_, decay)
        o = jnp.einsum('bhij,bhjv->bhiv', attn, v_) + jnp.einsum('bhik,bhkv->bhiv', q_ * jnp.exp(b_), state)
        b_last = b_[..., -1:, :]
        k_dec = k_ * jnp.exp(b_last - b_)
        state = jnp.exp(b_last[..., 0, :])[..., None] * state + jnp.einsum('bhjk,bhjv->bhkv', k_dec, v_)
        return state, o

    state0 = jnp.zeros((B, H, HK, HV), jnp.float32)
    _, o = lax.scan(step, state0, (qc, kc, vc, bc))
    return o.transpose(1, 0, 3, 2, 4).reshape(B, S, H, HV)


def causal_depthwise_conv(u, w, b):
    y = lax.conv_general_dilated(
        u, w[:, None, :].astype(u.dtype), window_strides=(1,), padding=((CONV_WIDTH - 1, 0),),
        dimension_numbers=('NWC', 'WIO', 'NWC'), feature_group_count=u.shape[-1])
    return y + b


def token_mixer(h, w_mix_in, gla_w_gate_up, gla_b_gate, gla_norm_g, gla_w_proj,
                conv_dw_w, conv_dw_b, conv_ln_g, conv_ln_b, conv_w_proj, w_mix_out):
    B, S, _ = h.shape
    z = h @ w_mix_in
    q, k, v, g_lr, og, cu, gates = jnp.split(z, SPLIT_POINTS, axis=-1)

    q = q.reshape(B, S, GLA_HEADS, GLA_HK).astype(jnp.float32) * (GLA_HK ** -0.5)
    k = k.reshape(B, S, GLA_HEADS, GLA_HK).astype(jnp.float32)
    v = v.reshape(B, S, GLA_HEADS, GLA_HV).astype(jnp.float32)
    gate_logit = (g_lr @ gla_w_gate_up + gla_b_gate).astype(jnp.float32)
    log_a = (jax.nn.log_sigmoid(gate_logit) / GLA_GATE_NORMALIZER).reshape(B, S, GLA_HEADS, GLA_HK)
    o = gla_chunked(q, k, v, log_a)
    o = rms_norm(o, gla_norm_g.astype(jnp.float32)).reshape(B, S, GLA_DV).astype(h.dtype)
    y_gla = (o * jax.nn.silu(og)) @ gla_w_proj

    ua, ub = jnp.split(cu, 2, axis=-1)
    u = ua * jax.nn.sigmoid(ub)
    u = causal_depthwise_conv(u, conv_dw_w, conv_dw_b)
    u = jax.nn.silu(layer_norm(u, conv_ln_g, conv_ln_b))
    y_conv = u @ conv_w_proj

    gate_a, gate_b = jnp.split(jax.nn.sigmoid(gates), 2, axis=-1)
    return (gate_a * y_gla + gate_b * y_conv) @ w_mix_out


def setup_inputs(seed: int = 0) -> dict:
    key = jax.random.key(seed)
    ks = jax.random.split(key, 24)
    f32 = jnp.float32
    L = DEPTH

    def nrm(k, shape, scale):
        return jax.random.normal(k, shape, f32) * scale

    def gain(k, shape):
        return 1.0 + 0.05 * jax.random.normal(k, shape, f32)

    return {
        "x": jax.random.normal(ks[0], (BATCH, SEQ, D_MODEL), f32),
        "ffn1_pre_g": gain(ks[1], (L, D_MODEL)),
        "ffn1_w_in": nrm(ks[2], (L, D_MODEL, 2 * D_FF), D_MODEL ** -0.5),
        "ffn1_w_out": nrm(ks[3], (L, D_FF, D_MODEL), D_FF ** -0.5),
        "ffn1_post_g": gain(ks[4], (L, D_MODEL)),
        "mix_pre_g": gain(ks[5], (L, D_MODEL)),
        "w_mix_in": nrm(ks[6], (L, D_MODEL, MIX_IN_WIDTH), D_MODEL ** -0.5),
        "gla_w_gate_up": nrm(ks[7], (L, GLA_GATE_RANK, GLA_DK), GLA_GATE_RANK ** -0.5),
        "gla_b_gate": nrm(ks[8], (L, GLA_DK), 0.1),
        "gla_norm_g": gain(ks[9], (L, GLA_HV)),
        "gla_w_proj": nrm(ks[10], (L, GLA_DV, D_MODEL), GLA_DV ** -0.5),
        "conv_dw_w": nrm(ks[11], (L, CONV_WIDTH, CONV_DIM), CONV_WIDTH ** -0.5),
        "conv_dw_b": nrm(ks[12], (L, CONV_DIM), 0.02),
        "conv_ln_g": gain(ks[13], (L, CONV_DIM)),
        "conv_ln_b": nrm(ks[14], (L, CONV_DIM), 0.02),
        "conv_w_proj": nrm(ks[15], (L, CONV_DIM, D_MODEL), CONV_DIM ** -0.5),
        "w_mix_out": nrm(ks[16], (L, D_MODEL, D_MODEL), D_MODEL ** -0.5),
        "mix_post_g": gain(ks[17], (L, D_MODEL)),
        "ffn2_pre_g": gain(ks[18], (L, D_MODEL)),
        "ffn2_w_in": nrm(ks[19], (L, D_MODEL, 2 * D_FF), D_MODEL ** -0.5),
        "ffn2_w_out": nrm(ks[20], (L, D_FF, D_MODEL), D_FF ** -0.5),
        "ffn2_post_g": gain(ks[21], (L, D_MODEL)),
    }


def reference(x, ffn1_pre_g, ffn1_w_in, ffn1_w_out, ffn1_post_g,
              mix_pre_g, w_mix_in, gla_w_gate_up, gla_b_gate, gla_norm_g, gla_w_proj,
              conv_dw_w, conv_dw_b, conv_ln_g, conv_ln_b, conv_w_proj, w_mix_out, mix_post_g,
              ffn2_pre_g, ffn2_w_in, ffn2_w_out, ffn2_post_g):
    for l in range(DEPTH):
        f = swiglu_ffn(rms_norm(x, ffn1_pre_g[l]), ffn1_w_in[l], ffn1_w_out[l])
        x = x + 0.5 * rms_norm(f, ffn1_post_g[l])
        m = token_mixer(rms_norm(x, mix_pre_g[l]), w_mix_in[l], gla_w_gate_up[l], gla_b_gate[l],
                        gla_norm_g[l], gla_w_proj[l], conv_dw_w[l], conv_dw_b[l], conv_ln_g[l],
                        conv_ln_b[l], conv_w_proj[l], w_mix_out[l])
        x = x + rms_norm(m, mix_post_g[l])
        f = swiglu_ffn(rms_norm(x, ffn2_pre_g[l]), ffn2_w_in[l], ffn2_w_out[l])
        x = x + 0.5 * rms_norm(f, ffn2_post_g[l])
    return x
```

```python
import functools

import jax
import jax.numpy as jnp
from jax import lax
from jax.experimental import pallas as pl
from jax.experimental.pallas import tpu as pltpu

D_MODEL = 1024
D_FF = 2816
GLA_HEADS = 4
GLA_HK = 128
GLA_HV = 256
GLA_DK = GLA_HEADS * GLA_HK
GLA_DV = GLA_HEADS * GLA_HV
GLA_GATE_RANK = 16
GLA_GATE_NORMALIZER = 16.0
CONV_WIDTH = 31
EPS = 1e-6

SUBLANES = 8
LANES = 128
VMEM_LIMIT_BYTES = 56 * 1024 * 1024

FFN_ROWS = 512
MIX_ROWS = 256
GLA_CHUNK = 128
CONV_TAIL = 32
CONV_ROW_BLOCK = 128
CONV_LANE_BLOCK = 128
RANK_PAD = LANES

_Q0, _K0, _V0 = 0, GLA_DK, 2 * GLA_DK
_OG0 = _V0 + GLA_DV
_UA0 = _OG0 + GLA_DV
_UB0 = _UA0 + D_MODEL
_GA0 = _UB0 + D_MODEL
_GB0 = _GA0 + D_MODEL
_GLR0 = _GB0 + D_MODEL
MIX_IN_PAD = _GLR0 + RANK_PAD

_F32 = jnp.float32
_BF16 = jnp.bfloat16


def _dot(a, b):
    return jnp.dot(a, b, preferred_element_type=_F32)


def _rms(x, g):
    return x * lax.rsqrt(jnp.mean(x * x, axis=-1, keepdims=True) + EPS) * g


def _ffn_kernel(x_ref, pre_g_ref, w_in_ref, w_out_ref, post_g_ref, o_ref):
    x = x_ref[...]
    h = _rms(x, pre_g_ref[...]).astype(_BF16)
    gate = _dot(h, w_in_ref[:, :D_FF])
    up = _dot(h, w_in_ref[:, D_FF:])
    act = (jax.nn.silu(gate) * up).astype(_BF16)
    f = _dot(act, w_out_ref[...])
    o_ref[...] = x + 0.5 * _rms(f, post_g_ref[...])


def _resident(shape):
    return pl.BlockSpec(shape, lambda *_: (0,) * len(shape), pipeline_mode=pl.Buffered(1))


def _ffn(x2d, pre_g, w_in, w_out, post_g):
    n_tok = x2d.shape[0]
    assert n_tok % FFN_ROWS == 0
    row_spec = pl.BlockSpec((FFN_ROWS, D_MODEL), lambda i: (i, 0))
    return pl.pallas_call(
        _ffn_kernel,
        grid=(n_tok // FFN_ROWS,),
        in_specs=[row_spec, _resident((1, D_MODEL)), _resident((D_MODEL, 2 * D_FF)),
                  _resident((D_FF, D_MODEL)), _resident((1, D_MODEL))],
        out_specs=row_spec,
        out_shape=jax.ShapeDtypeStruct(x2d.shape, _F32),
        compiler_params=pltpu.CompilerParams(
            dimension_semantics=("arbitrary",), vmem_limit_bytes=VMEM_LIMIT_BYTES),
        name="ffn_half_step",
    )(x2d, pre_g, w_in, w_out, post_g)


def _mixer_kernel(x_ref, pre_g_ref, w_in_ref, tril_ref, w_up_ref, b_gate_ref, norm_g_ref, w_gla_ref,
                  dw_w_ref, dw_b_ref, ln_g_ref, ln_b_ref, w_conv_ref, w_out_ref, post_g_ref,
                  o_ref, state_ref, ubuf_ref, obuf_ref, ybuf_ref):
    ts = MIX_ROWS

    @pl.when(pl.program_id(1) == 0)
    def _():
        state_ref[...] = jnp.zeros_like(state_ref)
        ubuf_ref[0:CONV_TAIL, :] = jnp.zeros((CONV_TAIL, D_MODEL), _F32)

    x = x_ref[...]
    h = _rms(x, pre_g_ref[...]).astype(_BF16)

    q = _dot(h, w_in_ref[:, _Q0:_K0]) * (GLA_HK ** -0.5)
    k = _dot(h, w_in_ref[:, _K0:_V0])
    v = _dot(h, w_in_ref[:, _V0:_OG0]).astype(_BF16)
    g_lr = _dot(h, w_in_ref[:, _GLR0:MIX_IN_PAD]).astype(_BF16)
    logit = _dot(g_lr, w_up_ref[...]) + b_gate_ref[...]
    log_a = jax.nn.log_sigmoid(logit) / GLA_GATE_NORMALIZER
    la_hi = log_a.astype(_BF16)
    la_lo = (log_a - la_hi.astype(_F32)).astype(_BF16)
    tril = tril_ref[...]
    b_all = _dot(tril, la_hi) + _dot(tril, la_lo)

    row = lax.broadcasted_iota(jnp.int32, (GLA_CHUNK, GLA_CHUNK), 0)
    col = lax.broadcasted_iota(jnp.int32, (GLA_CHUNK, GLA_CHUNK), 1)
    causal = col <= row
    mid = GLA_CHUNK // 2
    for c in range(ts // GLA_CHUNK):
        r0 = c * GLA_CHUNK
        b = b_all[r0:r0 + GLA_CHUNK]
        b_mid = b[mid - 1:mid]
        b_last = b[GLA_CHUNK - 1:GLA_CHUNK]
        q_c = q[r0:r0 + GLA_CHUNK]
        k_c = k[r0:r0 + GLA_CHUNK]
        e = b - b_mid
        qd = (q_c * jnp.exp(e)).astype(_BF16)
        kd = (k_c * jnp.exp(-e)).astype(_BF16)
        qb = (q_c * jnp.exp(b)).astype(_BF16)
        kl = (k_c * jnp.exp(b_last - b)).astype(_BF16)
        a_last = jnp.exp(b_last)
        for hd in range(GLA_HEADS):
            ks = slice(hd * GLA_HK, (hd + 1) * GLA_HK)
            vs = slice(hd * GLA_HV, (hd + 1) * GLA_HV)
            v_h = v[r0:r0 + GLA_CHUNK, vs]
            attn = lax.dot_general(qd[:, ks], kd[:, ks], (((1,), (1,)), ((), ())),
                                   preferred_element_type=_F32)
            attn = jnp.where(causal, attn, 0.0).astype(_BF16)
            s_h = state_ref[hd]
            obuf_ref[r0:r0 + GLA_CHUNK, vs] = _dot(attn, v_h) + _dot(qb[:, ks], s_h.astype(_BF16))
            a_col = jnp.broadcast_to(a_last[:, ks], (GLA_HK, GLA_HK)).T
            a_col = jnp.concatenate([a_col] * (GLA_HV // GLA_HK), axis=1)
            state_ref[hd] = s_h * a_col + lax.dot_general(
                kl[:, ks], v_h, (((0,), (0,)), ((), ())), preferred_element_type=_F32)

    og = _dot(h, w_in_ref[:, _OG0:_UA0])
    sil_og = jax.nn.silu(og)
    norm_g = norm_g_ref[...]
    heads = []
    for hd in range(GLA_HEADS):
        vs = slice(hd * GLA_HV, (hd + 1) * GLA_HV)
        heads.append((_rms(obuf_ref[:, vs], norm_g) * sil_og[:, vs]).astype(_BF16))
    y_gla = _dot(jnp.concatenate(heads, axis=1), w_gla_ref[...])

    ua = _dot(h, w_in_ref[:, _UA0:_UB0])
    ub = _dot(h, w_in_ref[:, _UB0:_GA0])
    ubuf_ref[CONV_TAIL:CONV_TAIL + ts, :] = ua * jax.nn.sigmoid(ub)
    tap0 = CONV_TAIL - (CONV_WIDTH - 1)
    win = CONV_ROW_BLOCK + CONV_TAIL
    for rb in range(ts // CONV_ROW_BLOCK):
        for cb in range(D_MODEL // CONV_LANE_BLOCK):
            cs = slice(cb * CONV_LANE_BLOCK, (cb + 1) * CONV_LANE_BLOCK)
            xw = ubuf_ref[rb * CONV_ROW_BLOCK:rb * CONV_ROW_BLOCK + win, cs]
            acc = jnp.broadcast_to(dw_b_ref[:, cs], (CONV_ROW_BLOCK, CONV_LANE_BLOCK))
            for r in range(SUBLANES):
                xr = xw if r == 0 else pltpu.roll(xw, win - r, axis=0)
                for a in range(win // SUBLANES):
                    tau = SUBLANES * a + r - tap0
                    if 0 <= tau < CONV_WIDTH:
                        acc = acc + xr[SUBLANES * a:SUBLANES * a + CONV_ROW_BLOCK] * dw_w_ref[tau:tau + 1, cs]
            ybuf_ref[rb * CONV_ROW_BLOCK:(rb + 1) * CONV_ROW_BLOCK, cs] = acc
    ubuf_ref[0:CONV_TAIL, :] = ubuf_ref[ts:ts + CONV_TAIL, :]

    yc = ybuf_ref[...]
    mu = jnp.mean(yc, axis=-1, keepdims=True)
    yc0 = yc - mu
    var = jnp.mean(yc0 * yc0, axis=-1, keepdims=True)
    ln = yc0 * lax.rsqrt(var + EPS) * ln_g_ref[...] + ln_b_ref[...]
    y_conv = _dot(jax.nn.silu(ln).astype(_BF16), w_conv_ref[...])

    gate_a = jax.nn.sigmoid(_dot(h, w_in_ref[:, _GA0:_GB0]))
    gate_b = jax.nn.sigmoid(_dot(h, w_in_ref[:, _GB0:_GLR0]))
    m = _dot((gate_a * y_gla + gate_b * y_conv).astype(_BF16), w_out_ref[...])
    o_ref[...] = x + _rms(m, post_g_ref[...])


def _mixer(x, pre_g, w_in, tril, w_up, b_gate, norm_g, w_gla, dw_w, dw_b, ln_g, ln_b, w_conv, w_out, post_g):
    batch, seq, _ = x.shape
    ts = MIX_ROWS
    assert seq % ts == 0 and ts % GLA_CHUNK == 0 and ts % CONV_ROW_BLOCK == 0
    tok_spec = pl.BlockSpec((None, ts, D_MODEL), lambda b, s: (b, s, 0))
    consts = (pre_g, w_in, tril, w_up, b_gate, norm_g, w_gla, dw_w, dw_b, ln_g, ln_b, w_conv, w_out, post_g)
    return pl.pallas_call(
        _mixer_kernel,
        grid=(batch, seq // ts),
        in_specs=[tok_spec] + [_resident(c.shape) for c in consts],
        out_specs=tok_spec,
        out_shape=jax.ShapeDtypeStruct(x.shape, _F32),
        scratch_shapes=[
            pltpu.VMEM((GLA_HEADS, GLA_HK, GLA_HV), _F32),
            pltpu.VMEM((CONV_TAIL + ts, D_MODEL), _F32),
            pltpu.VMEM((ts, GLA_DV), _F32),
            pltpu.VMEM((ts, D_MODEL), _F32),
        ],
        compiler_params=pltpu.CompilerParams(
            dimension_semantics=("arbitrary", "arbitrary"), vmem_limit_bytes=VMEM_LIMIT_BYTES),
        name="gla_conv_mixer",
    )(x, *consts)


def _chunk_tril(n, chunk):
    i = jnp.arange(n)
    same = (i[:, None] // chunk) == (i[None, :] // chunk)
    return (same & (i[None, :] <= i[:, None])).astype(_BF16)


def kernel(x, ffn1_pre_g, ffn1_w_in, ffn1_w_out, ffn1_post_g, mix_pre_g, w_mix_in, gla_w_gate_up, gla_b_gate,
           gla_norm_g, gla_w_proj, conv_dw_w, conv_dw_b, conv_ln_g, conv_ln_b, conv_w_proj, w_mix_out,
           mix_post_g, ffn2_pre_g, ffn2_w_in, ffn2_w_out, ffn2_post_g):
    batch, seq, d = x.shape
    depth = ffn1_pre_g.shape[0]
    tril = _chunk_tril(MIX_ROWS, GLA_CHUNK)
    glr0 = 2 * GLA_DK + GLA_DV
    for l in range(depth):
        row = lambda p: p[l].reshape(1, -1)
        x2d = _ffn(x.reshape(batch * seq, d), row(ffn1_pre_g), ffn1_w_in[l].astype(_BF16),
                   ffn1_w_out[l].astype(_BF16), row(ffn1_post_g))
        w = w_mix_in[l]
        w_glr = jnp.pad(w[:, glr0:glr0 + GLA_GATE_RANK], ((0, 0), (0, RANK_PAD - GLA_GATE_RANK)))
        w_in = jnp.concatenate([w[:, :glr0], w[:, glr0 + GLA_GATE_RANK:], w_glr], axis=1).astype(_BF16)
        w_up = jnp.pad(gla_w_gate_up[l], ((0, RANK_PAD - GLA_GATE_RANK), (0, 0))).astype(_BF16)
        x = _mixer(x2d.reshape(batch, seq, d), row(mix_pre_g), w_in, tril, w_up, row(gla_b_gate),
                   row(gla_norm_g), gla_w_proj[l].astype(_BF16), conv_dw_w[l], row(conv_dw_b),
                   row(conv_ln_g), row(conv_ln_b), conv_w_proj[l].astype(_BF16),
                   w_mix_out[l].astype(_BF16), row(mix_post_g))
        x2d = _ffn(x.reshape(batch * seq, d), row(ffn2_pre_g), ffn2_w_in[l].astype(_BF16),
                   ffn2_w_out[l].astype(_BF16), row(ffn2_post_g))
        x = x2d.reshape(batch, seq, d)
    return x
```

```python
import itertools

import jax
import jax.numpy as jnp
from jax import lax
from jax.experimental import pallas as pl
from jax.experimental.pallas import tpu as pltpu

D_MODEL = 1024
D_FF = 2816
GLA_HEADS = 4
GLA_HK = 128
GLA_HV = 256
GLA_DK = GLA_HEADS * GLA_HK
GLA_DV = GLA_HEADS * GLA_HV
GLA_GATE_RANK = 16
GLA_GATE_NORMALIZER = 16.0
CONV_WIDTH = 31
EPS = 1e-6

SUBLANES = 8
LANES = 128
MXU_TILE = 256
VMEM_LIMIT_BYTES = 56 * 1024 * 1024

FFN_ROWS = 512
MIX_ROWS = 512
GLA_CHUNK = 128
ROW_BLOCK = 128
CONV_TAIL = 32
CONV_LANE_BLOCK = 128
RANK_PAD = LANES

_UA0 = 0
_UB0 = _UA0 + D_MODEL
_Z0 = _UB0 + D_MODEL
_Q0 = _Z0
_K0 = _Q0 + GLA_DK
_V0 = _K0 + GLA_DK
_OG0 = _V0 + GLA_DV
_GA0 = _OG0 + GLA_DV
_GB0 = _GA0 + D_MODEL
_GLR0 = _GB0 + D_MODEL
MIX_IN_PAD = _GLR0 + RANK_PAD

_F32 = jnp.float32
_BF16 = jnp.bfloat16


def _dot(a, b):
    return jnp.dot(a, b, preferred_element_type=_F32)


def _rms(x, g):
    return x * lax.rsqrt(jnp.mean(x * x, axis=-1, keepdims=True) + EPS) * g


def _interleave(*task_lists):
    longest = max(len(t) for t in task_lists)
    done = [0] * len(task_lists)
    for step in range(1, longest + 1):
        for i, tasks in enumerate(task_lists):
            upto = (step * len(tasks) + longest - 1) // longest
            for task in tasks[done[i]:upto]:
                task()
            done[i] = upto


def _ffn_kernel(x_ref, pre_g_ref, w_in_ref, w_out_ref, post_g_ref, o_ref):
    x = x_ref[...]
    h = _rms(x, pre_g_ref[...]).astype(_BF16)
    gate = _dot(h, w_in_ref[:, :D_FF])
    up = _dot(h, w_in_ref[:, D_FF:])
    act = (jax.nn.silu(gate) * up).astype(_BF16)
    f = _dot(act, w_out_ref[...])
    o_ref[...] = x + 0.5 * _rms(f, post_g_ref[...])


def _resident(shape):
    return pl.BlockSpec(shape, lambda *_: (0,) * len(shape), pipeline_mode=pl.Buffered(1))


def _ffn(x2d, pre_g, w_in, w_out, post_g):
    n_tok = x2d.shape[0]
    assert n_tok % FFN_ROWS == 0
    row_spec = pl.BlockSpec((FFN_ROWS, D_MODEL), lambda i: (i, 0))
    return pl.pallas_call(
        _ffn_kernel,
        grid=(n_tok // FFN_ROWS,),
        in_specs=[row_spec, _resident((1, D_MODEL)), _resident((D_MODEL, 2 * D_FF)),
                  _resident((D_FF, D_MODEL)), _resident((1, D_MODEL))],
        out_specs=row_spec,
        out_shape=jax.ShapeDtypeStruct(x2d.shape, _F32),
        compiler_params=pltpu.CompilerParams(
            dimension_semantics=("arbitrary",), vmem_limit_bytes=VMEM_LIMIT_BYTES),
        name="ffn_half_step",
    )(x2d, pre_g, w_in, w_out, post_g)


def _mixer_kernel(x_ref, pre_g_ref, w_in_ref, tril_ref, w_up_ref, b_gate_ref, norm_g_ref, w_gla_ref,
                  dw_w_ref, dw_b_ref, ln_g_ref, ln_b_ref, w_conv_ref, w_out_ref, post_g_ref,
                  o_ref, state_ref, ubuf_ref, zbuf_ref, obuf_ref, ybuf_ref, cbuf_ref):
    ts = MIX_ROWS
    row_blocks = [slice(r, r + ROW_BLOCK) for r in range(0, ts, ROW_BLOCK)]

    def zcols(c0, c1):
        return slice(c0 - _Z0, c1 - _Z0)

    @pl.when(pl.program_id(1) == 0)
    def _():
        state_ref[...] = jnp.zeros_like(state_ref)
        ubuf_ref[0:CONV_TAIL, :] = jnp.zeros((CONV_TAIL, D_MODEL), _F32)

    h = _rms(x_ref[...], pre_g_ref[...]).astype(_BF16)

    ua = _dot(h, w_in_ref[:, _UA0:_UB0])
    ub = _dot(h, w_in_ref[:, _UB0:_Z0])
    ubuf_ref[CONV_TAIL:CONV_TAIL + ts, :] = ua * jax.nn.sigmoid(ub)

    tap0 = CONV_TAIL - (CONV_WIDTH - 1)
    win = ROW_BLOCK + CONV_TAIL

    def conv_block(rows, cs):
        xw = ubuf_ref[rows.start:rows.start + win, cs]
        acc = jnp.broadcast_to(dw_b_ref[:, cs], (ROW_BLOCK, CONV_LANE_BLOCK))
        for r in range(SUBLANES):
            xr = xw if r == 0 else pltpu.roll(xw, win - r, axis=0)
            for a in range(win // SUBLANES):
                tau = SUBLANES * a + r - tap0
                if 0 <= tau < CONV_WIDTH:
                    acc = acc + xr[SUBLANES * a:SUBLANES * a + ROW_BLOCK] * dw_w_ref[tau:tau + 1, cs]
        ybuf_ref[rows, cs] = acc

    def proj_piece(c0, c1):
        zbuf_ref[:, zcols(c0, c1)] = _dot(h, w_in_ref[:, c0:c1])

    conv_tasks = [lambda rows=rows, c=c: conv_block(rows, slice(c, c + CONV_LANE_BLOCK))
                  for rows in row_blocks for c in range(0, D_MODEL, CONV_LANE_BLOCK)]
    proj_tasks = [lambda c=c: proj_piece(c, min(c + MXU_TILE, MIX_IN_PAD))
                  for c in range(_Z0, MIX_IN_PAD, MXU_TILE)]
    _interleave(conv_tasks, proj_tasks)
    ubuf_ref[0:CONV_TAIL, :] = ubuf_ref[ts:ts + CONV_TAIL, :]

    g_lr = zbuf_ref[:, zcols(_GLR0, MIX_IN_PAD)].astype(_BF16)
    logit = _dot(g_lr, w_up_ref[...]) + b_gate_ref[...]
    log_a = jax.nn.log_sigmoid(logit) / GLA_GATE_NORMALIZER
    la_hi = log_a.astype(_BF16)
    la_lo = (log_a - la_hi.astype(_F32)).astype(_BF16)
    tril = tril_ref[...]
    b_all = _dot(tril, la_hi) + _dot(tril, la_lo)

    row = lax.broadcasted_iota(jnp.int32, (GLA_CHUNK, GLA_CHUNK), 0)
    col = lax.broadcasted_iota(jnp.int32, (GLA_CHUNK, GLA_CHUNK), 1)
    causal = col <= row
    mid = GLA_CHUNK // 2

    def gla_chunk_head(c, hd):
        rows = slice(c * GLA_CHUNK, (c + 1) * GLA_CHUNK)
        ks = slice(hd * GLA_HK, (hd + 1) * GLA_HK)
        vs = slice(hd * GLA_HV, (hd + 1) * GLA_HV)
        b = b_all[rows, ks]
        b_mid = b[mid - 1:mid]
        b_last = b[GLA_CHUNK - 1:GLA_CHUNK]
        q_c = zbuf_ref[rows, zcols(_Q0 + ks.start, _Q0 + ks.stop)] * (GLA_HK ** -0.5)
        k_c = zbuf_ref[rows, zcols(_K0 + ks.start, _K0 + ks.stop)]
        v_h = zbuf_ref[rows, zcols(_V0 + vs.start, _V0 + vs.stop)].astype(_BF16)
        e = b - b_mid
        qd = (q_c * jnp.exp(e)).astype(_BF16)
        kd = (k_c * jnp.exp(-e)).astype(_BF16)
        qb = (q_c * jnp.exp(b)).astype(_BF16)
        kl = (k_c * jnp.exp(b_last - b)).astype(_BF16)
        attn = lax.dot_general(qd, kd, (((1,), (1,)), ((), ())), preferred_element_type=_F32)
        attn = jnp.where(causal, attn, 0.0).astype(_BF16)
        s_h = state_ref[hd]
        obuf_ref[rows, vs] = _dot(attn, v_h) + _dot(qb, s_h.astype(_BF16))
        a_col = jnp.broadcast_to(jnp.exp(b_last), (GLA_HK, GLA_HK)).T
        a_col = jnp.concatenate([a_col] * (GLA_HV // GLA_HK), axis=1)
        state_ref[hd] = s_h * a_col + lax.dot_general(
            kl, v_h, (((0,), (0,)), ((), ())), preferred_element_type=_F32)

    def ln_silu(rows):
        yc = ybuf_ref[rows, :]
        yc0 = yc - jnp.mean(yc, axis=-1, keepdims=True)
        var = jnp.mean(yc0 * yc0, axis=-1, keepdims=True)
        ln = yc0 * lax.rsqrt(var + EPS) * ln_g_ref[...] + ln_b_ref[...]
        cbuf_ref[rows, :] = jax.nn.silu(ln).astype(_BF16)

    def in_place(fn, rows, c0):
        cols = zcols(c0, c0 + D_MODEL)
        zbuf_ref[rows, cols] = fn(zbuf_ref[rows, cols])

    gla_tasks = [lambda c=c, hd=hd: gla_chunk_head(c, hd)
                 for c in range(ts // GLA_CHUNK) for hd in range(GLA_HEADS)]
    vpu_tasks = list(itertools.chain.from_iterable(
        (lambda rows=rows: ln_silu(rows),
         lambda rows=rows: in_place(jax.nn.silu, rows, _OG0),
         lambda rows=rows: in_place(jax.nn.sigmoid, rows, _GA0),
         lambda rows=rows: in_place(jax.nn.sigmoid, rows, _GB0)) for rows in row_blocks))
    _interleave(gla_tasks, vpu_tasks)

    y_conv = _dot(cbuf_ref[...], w_conv_ref[...])
    norm_g = norm_g_ref[...]
    heads = []
    for hd in range(GLA_HEADS):
        vs = slice(hd * GLA_HV, (hd + 1) * GLA_HV)
        sil_og = zbuf_ref[:, zcols(_OG0 + vs.start, _OG0 + vs.stop)]
        heads.append((_rms(obuf_ref[:, vs], norm_g) * sil_og).astype(_BF16))
    y_gla = _dot(jnp.concatenate(heads, axis=1), w_gla_ref[...])

    gate_a = zbuf_ref[:, zcols(_GA0, _GB0)]
    gate_b = zbuf_ref[:, zcols(_GB0, _GLR0)]
    m = _dot((gate_a * y_gla + gate_b * y_conv).astype(_BF16), w_out_ref[...])
    o_ref[...] = x_ref[...] + _rms(m, post_g_ref[...])


def _mixer(x, pre_g, w_in, tril, w_up, b_gate, norm_g, w_gla, dw_w, dw_b, ln_g, ln_b, w_conv, w_out, post_g):
    batch, seq, _ = x.shape
    ts = MIX_ROWS
    assert seq % ts == 0 and ts % GLA_CHUNK == 0 and ts % ROW_BLOCK == 0
    tok_spec = pl.BlockSpec((None, ts, D_MODEL), lambda b, s: (b, s, 0))
    consts = (pre_g, w_in, tril, w_up, b_gate, norm_g, w_gla, dw_w, dw_b, ln_g, ln_b, w_conv, w_out, post_g)
    return pl.pallas_call(
        _mixer_kernel,
        grid=(batch, seq // ts),
        in_specs=[tok_spec] + [_resident(c.shape) for c in consts],
        out_specs=tok_spec,
        out_shape=jax.ShapeDtypeStruct(x.shape, _F32),
        scratch_shapes=[
            pltpu.VMEM((GLA_HEADS, GLA_HK, GLA_HV), _F32),
            pltpu.VMEM((CONV_TAIL + ts, D_MODEL), _F32),
            pltpu.VMEM((ts, MIX_IN_PAD - _Z0), _F32),
            pltpu.VMEM((ts, GLA_DV), _F32),
            pltpu.VMEM((ts, D_MODEL), _F32),
            pltpu.VMEM((ts, D_MODEL), _BF16),
        ],
        compiler_params=pltpu.CompilerParams(
            dimension_semantics=("arbitrary", "arbitrary"), vmem_limit_bytes=VMEM_LIMIT_BYTES),
        name="gla_conv_mixer",
    )(x, *consts)


def _chunk_tril(n, chunk):
    i = jnp.arange(n)
    same = (i[:, None] // chunk) == (i[None, :] // chunk)
    return (same & (i[None, :] <= i[:, None])).astype(_BF16)


def _mixer_in_weights(w):
    q_end = 2 * GLA_DK + GLA_DV
    lr_end = q_end + GLA_GATE_RANK
    og_end = lr_end + GLA_DV
    cu_end = og_end + 2 * D_MODEL
    w_glr = jnp.pad(w[:, q_end:lr_end], ((0, 0), (0, RANK_PAD - GLA_GATE_RANK)))
    return jnp.concatenate([w[:, og_end:cu_end], w[:, :q_end], w[:, lr_end:og_end], w[:, cu_end:], w_glr],
                           axis=1).astype(_BF16)


def kernel(x, ffn1_pre_g, ffn1_w_in, ffn1_w_out, ffn1_post_g, mix_pre_g, w_mix_in, gla_w_gate_up, gla_b_gate,
           gla_norm_g, gla_w_proj, conv_dw_w, conv_dw_b, conv_ln_g, conv_ln_b, conv_w_proj, w_mix_out,
           mix_post_g, ffn2_pre_g, ffn2_w_in, ffn2_w_out, ffn2_post_g):
    batch, seq, d = x.shape
    depth = ffn1_pre_g.shape[0]
    tril = _chunk_tril(MIX_ROWS, GLA_CHUNK)
    for l in range(depth):
        row = lambda p: p[l].reshape(1, -1)
        x2d = _ffn(x.reshape(batch * seq, d), row(ffn1_pre_g), ffn1_w_in[l].astype(_BF16),
                   ffn1_w_out[l].astype(_BF16), row(ffn1_post_g))
        w_up = jnp.pad(gla_w_gate_up[l], ((0, RANK_PAD - GLA_GATE_RANK), (0, 0))).astype(_BF16)
        x = _mixer(x2d.reshape(batch, seq, d), row(mix_pre_g), _mixer_in_weights(w_mix_in[l]), tril, w_up,
                   row(gla_b_gate), row(gla_norm_g), gla_w_proj[l].astype(_BF16), conv_dw_w[l],
                   row(conv_dw_b), row(conv_ln_g), row(conv_ln_b), conv_w_proj[l].astype(_BF16),
                   w_mix_out[l].astype(_BF16), row(mix_post_g))
        x2d = _ffn(x.reshape(batch * seq, d), row(ffn2_pre_g), ffn2_w_in[l].astype(_BF16),
                   ffn2_w_out[l].astype(_BF16), row(ffn2_post_g))
        x = x2d.reshape(batch, seq, d)
    return x
```

```python
import functools
import itertools

import jax
import jax.numpy as jnp
from jax import lax
from jax.experimental import pallas as pl
from jax.experimental.pallas import tpu as pltpu

D_MODEL = 1024
D_FF = 2816
GLA_HEADS = 4
GLA_HK = 128
GLA_HV = 256
GLA_DK = GLA_HEADS * GLA_HK
GLA_DV = GLA_HEADS * GLA_HV
GLA_GATE_RANK = 16
GLA_GATE_NORMALIZER = 16.0
CONV_WIDTH = 31
EPS = 1e-6

SUBLANES = 8
LANES = 128
MXU_TILE = 256
VMEM_LIMIT_BYTES = 56 * 1024 * 1024

FFN_ROWS = 512
MIX_ROWS = 512
GLA_CHUNK = 128
ROW_BLOCK = 128
CONV_TAIL = 32
CONV_LANE_BLOCK = 128
RANK_PAD = LANES
GLA_FACTOR_LOG_RANGE = 60.0

_UA0 = 0
_UB0 = _UA0 + D_MODEL
_Z0 = _UB0 + D_MODEL
_Q0 = _Z0
_K0 = _Q0 + GLA_DK
_V0 = _K0 + GLA_DK
_OG0 = _V0 + GLA_DV
_GA0 = _OG0 + GLA_DV
_GB0 = _GA0 + D_MODEL
_GLR0 = _GB0 + D_MODEL
MIX_IN_PAD = _GLR0 + RANK_PAD

_F32 = jnp.float32
_BF16 = jnp.bfloat16


def _dot(a, b):
    return jnp.dot(a, b, preferred_element_type=_F32)


def _rms(x, g):
    return x * lax.rsqrt(jnp.mean(x * x, axis=-1, keepdims=True) + EPS) * g


def _interleave(*task_lists):
    longest = max(len(t) for t in task_lists)
    done = [0] * len(task_lists)
    for step in range(1, longest + 1):
        for i, tasks in enumerate(task_lists):
            upto = (step * len(tasks) + longest - 1) // longest
            for task in tasks[done[i]:upto]:
                task()
            done[i] = upto


def _ffn_kernel(x_ref, pre_g_ref, w_in_ref, w_out_ref, post_g_ref, o_ref):
    x = x_ref[...]
    h = _rms(x, pre_g_ref[...]).astype(_BF16)
    gate = _dot(h, w_in_ref[:, :D_FF])
    up = _dot(h, w_in_ref[:, D_FF:])
    act = (jax.nn.silu(gate) * up).astype(_BF16)
    f = _dot(act, w_out_ref[...])
    o_ref[...] = x + 0.5 * _rms(f, post_g_ref[...])


def _resident(shape):
    return pl.BlockSpec(shape, lambda *_: (0,) * len(shape), pipeline_mode=pl.Buffered(1))


def _ffn(x2d, pre_g, w_in, w_out, post_g):
    n_tok = x2d.shape[0]
    assert n_tok % FFN_ROWS == 0
    row_spec = pl.BlockSpec((FFN_ROWS, D_MODEL), lambda i: (i, 0))
    return pl.pallas_call(
        _ffn_kernel,
        grid=(n_tok // FFN_ROWS,),
        in_specs=[row_spec, _resident((1, D_MODEL)), _resident((D_MODEL, 2 * D_FF)),
                  _resident((D_FF, D_MODEL)), _resident((1, D_MODEL))],
        out_specs=row_spec,
        out_shape=jax.ShapeDtypeStruct(x2d.shape, _F32),
        compiler_params=pltpu.CompilerParams(
            dimension_semantics=("arbitrary",), vmem_limit_bytes=VMEM_LIMIT_BYTES),
        name="ffn_half_step",
    )(x2d, pre_g, w_in, w_out, post_g)


def _mixer_kernel(robust, x_ref, pre_g_ref, w_in_ref, tril_ref, w_up_ref, b_gate_ref, norm_g_ref, w_gla_ref,
                  dw_w_ref, dw_b_ref, ln_g_ref, ln_b_ref, w_conv_ref, w_out_ref, post_g_ref, o_ref, *rest):
    if robust:
        state_ref, ubuf_ref, zbuf_ref, obuf_ref, ybuf_ref, cbuf_ref, ball_ref, attn_ref, h_ref, kbuf_ref = rest
    else:
        decay_ref, state_ref, ubuf_ref, zbuf_ref, obuf_ref, ybuf_ref, cbuf_ref = rest
    ts = MIX_ROWS
    n_chunks = ts // GLA_CHUNK
    row_blocks = [slice(r, r + ROW_BLOCK) for r in range(0, ts, ROW_BLOCK)]

    def zcols(c0, c1):
        return slice(c0 - _Z0, c1 - _Z0)

    @pl.when(pl.program_id(1) == 0)
    def _():
        state_ref[...] = jnp.zeros_like(state_ref)
        ubuf_ref[0:CONV_TAIL, :] = jnp.zeros((CONV_TAIL, D_MODEL), _F32)
        if robust:
            attn_ref[...] = jnp.zeros_like(attn_ref)

    h = _rms(x_ref[...], pre_g_ref[...]).astype(_BF16)

    def cumulative_log_decay():
        g_lr = _dot(h, w_in_ref[:, _GLR0:MIX_IN_PAD]).astype(_BF16)
        logit = _dot(g_lr, w_up_ref[...]) + b_gate_ref[...]
        log_a = jax.nn.log_sigmoid(logit) / GLA_GATE_NORMALIZER
        la_hi = log_a.astype(_BF16)
        la_lo = (log_a - la_hi.astype(_F32)).astype(_BF16)
        tril = tril_ref[...]
        return _dot(tril, la_hi) + _dot(tril, la_lo)

    if robust:
        h_ref[...] = h
        ball_ref[...] = cumulative_log_decay()
        decay_in_range = jnp.min(ball_ref[...]) >= -GLA_FACTOR_LOG_RANGE

        @pl.when(jnp.logical_not(decay_in_range))
        def _():
            col_id = lax.broadcasted_iota(jnp.int32, (GLA_CHUNK, GLA_CHUNK), 1)
            for hd in range(GLA_HEADS):
                ks = slice(hd * GLA_HK, (hd + 1) * GLA_HK)
                kbuf_ref[...] = _dot(h_ref[...], w_in_ref[:, _K0 + ks.start:_K0 + ks.stop])
                for c in range(n_chunks):
                    rows = slice(c * GLA_CHUNK, (c + 1) * GLA_CHUNK)
                    q_c = _dot(h_ref[rows, :], w_in_ref[:, _Q0 + ks.start:_Q0 + ks.stop]) * (GLA_HK ** -0.5)
                    b_c = ball_ref[rows, ks]

                    def columns(jb, acc, rows=rows, ks=ks, q_c=q_c, b_c=b_c):
                        src = pl.ds(pl.multiple_of(rows.start + jb * SUBLANES, SUBLANES), SUBLANES)
                        b_j, k_j = ball_ref[src, ks], kbuf_ref[src, :]
                        for r in range(SUBLANES):
                            w = jnp.exp(jnp.minimum(b_c - b_j[r:r + 1], 0.0))
                            a_j = jnp.sum(q_c * k_j[r:r + 1] * w, axis=-1, keepdims=True)
                            acc = acc + jnp.where(col_id == jb * SUBLANES + r, a_j, 0.0)
                        return acc

                    attn_ref[c, hd] = lax.fori_loop(0, GLA_CHUNK // SUBLANES, columns,
                                                    jnp.zeros((GLA_CHUNK, GLA_CHUNK), _F32))

        h = h_ref[...]
        b_all = ball_ref[...]

    ua = _dot(h, w_in_ref[:, _UA0:_UB0])
    ub = _dot(h, w_in_ref[:, _UB0:_Z0])
    ubuf_ref[CONV_TAIL:CONV_TAIL + ts, :] = ua * jax.nn.sigmoid(ub)

    tap0 = CONV_TAIL - (CONV_WIDTH - 1)
    win = ROW_BLOCK + CONV_TAIL

    def conv_block(rows, cs):
        xw = ubuf_ref[rows.start:rows.start + win, cs]
        acc = jnp.broadcast_to(dw_b_ref[:, cs], (ROW_BLOCK, CONV_LANE_BLOCK))
        for r in range(SUBLANES):
            xr = xw if r == 0 else pltpu.roll(xw, win - r, axis=0)
            for a in range(win // SUBLANES):
                tau = SUBLANES * a + r - tap0
                if 0 <= tau < CONV_WIDTH:
                    acc = acc + xr[SUBLANES * a:SUBLANES * a + ROW_BLOCK] * dw_w_ref[tau:tau + 1, cs]
        ybuf_ref[rows, cs] = acc

    def proj_piece(c0):
        zbuf_ref[:, zcols(c0, c0 + MXU_TILE)] = _dot(h, w_in_ref[:, c0:c0 + MXU_TILE])

    conv_tasks = [lambda rows=rows, c=c: conv_block(rows, slice(c, c + CONV_LANE_BLOCK))
                  for rows in row_blocks for c in range(0, D_MODEL, CONV_LANE_BLOCK)]
    proj_tasks = [lambda c=c: proj_piece(c) for c in range(_Z0, _GLR0, MXU_TILE)]
    _interleave(conv_tasks, proj_tasks)
    ubuf_ref[0:CONV_TAIL, :] = ubuf_ref[ts:ts + CONV_TAIL, :]

    if not robust:
        b_all = cumulative_log_decay()
        tile_min = jnp.min(b_all, axis=0, keepdims=True)
        tile_min = functools.reduce(jnp.minimum, [tile_min[:, c:c + LANES] for c in range(0, GLA_DK, LANES)])
        decay_ref[...] = jnp.broadcast_to(tile_min, (SUBLANES, LANES))

    def chunk_slices(c, hd):
        rows = slice(c * GLA_CHUNK, (c + 1) * GLA_CHUNK)
        ks = slice(hd * GLA_HK, (hd + 1) * GLA_HK)
        vs = slice(hd * GLA_HV, (hd + 1) * GLA_HV)
        return (rows, ks, vs, zcols(_Q0 + ks.start, _Q0 + ks.stop), zcols(_K0 + ks.start, _K0 + ks.stop),
                zcols(_V0 + vs.start, _V0 + vs.stop))

    row = lax.broadcasted_iota(jnp.int32, (GLA_CHUNK, GLA_CHUNK), 0)
    col = lax.broadcasted_iota(jnp.int32, (GLA_CHUNK, GLA_CHUNK), 1)
    causal = col <= row
    mid = GLA_CHUNK // 2

    def gla_chunk_head(c, hd):
        rows, ks, vs, qc, kc, vc = chunk_slices(c, hd)
        b = b_all[rows, ks]
        b_mid = b[mid - 1:mid]
        b_last = b[GLA_CHUNK - 1:GLA_CHUNK]
        q_c = zbuf_ref[rows, qc] * (GLA_HK ** -0.5)
        k_c = zbuf_ref[rows, kc]
        v_h = zbuf_ref[rows, vc].astype(_BF16)
        e = b - b_mid
        qd = (q_c * jnp.exp(e)).astype(_BF16)
        kd = (k_c * jnp.exp(-e)).astype(_BF16)
        qb = (q_c * jnp.exp(b)).astype(_BF16)
        kl = (k_c * jnp.exp(b_last - b)).astype(_BF16)
        attn = lax.dot_general(qd, kd, (((1,), (1,)), ((), ())), preferred_element_type=_F32)
        if robust:
            attn = jnp.where(decay_in_range, attn, attn_ref[c, hd])
        attn = jnp.where(causal, attn, 0.0).astype(_BF16)
        s_h = state_ref[hd]
        obuf_ref[rows, vs] = _dot(attn, v_h) + _dot(qb, s_h.astype(_BF16))
        a_col = jnp.broadcast_to(jnp.exp(b_last), (GLA_HK, GLA_HK)).T
        a_col = jnp.concatenate([a_col] * (GLA_HV // GLA_HK), axis=1)
        state_ref[hd] = s_h * a_col + lax.dot_general(
            kl, v_h, (((0,), (0,)), ((), ())), preferred_element_type=_F32)

    def ln_silu(rows):
        yc = ybuf_ref[rows, :]
        yc0 = yc - jnp.mean(yc, axis=-1, keepdims=True)
        var = jnp.mean(yc0 * yc0, axis=-1, keepdims=True)
        ln = yc0 * lax.rsqrt(var + EPS) * ln_g_ref[...] + ln_b_ref[...]
        cbuf_ref[rows, :] = jax.nn.silu(ln).astype(_BF16)

    def in_place(fn, rows, c0):
        cols = zcols(c0, c0 + D_MODEL)
        zbuf_ref[rows, cols] = fn(zbuf_ref[rows, cols])

    gla_tasks = [lambda c=c, hd=hd: gla_chunk_head(c, hd) for c in range(n_chunks) for hd in range(GLA_HEADS)]
    vpu_tasks = list(itertools.chain.from_iterable(
        (lambda rows=rows: ln_silu(rows),
         lambda rows=rows: in_place(jax.nn.silu, rows, _OG0),
         lambda rows=rows: in_place(jax.nn.sigmoid, rows, _GA0),
         lambda rows=rows: in_place(jax.nn.sigmoid, rows, _GB0)) for rows in row_blocks))
    _interleave(gla_tasks, vpu_tasks)

    y_conv = _dot(cbuf_ref[...], w_conv_ref[...])
    norm_g = norm_g_ref[...]
    heads = []
    for hd in range(GLA_HEADS):
        vs = slice(hd * GLA_HV, (hd + 1) * GLA_HV)
        sil_og = zbuf_ref[:, zcols(_OG0 + vs.start, _OG0 + vs.stop)]
        heads.append((_rms(obuf_ref[:, vs], norm_g) * sil_og).astype(_BF16))
    y_gla = _dot(jnp.concatenate(heads, axis=1), w_gla_ref[...])

    gate_a = zbuf_ref[:, zcols(_GA0, _GB0)]
    gate_b = zbuf_ref[:, zcols(_GB0, _GLR0)]
    m = _dot((gate_a * y_gla + gate_b * y_conv).astype(_BF16), w_out_ref[...])
    o_ref[...] = x_ref[...] + _rms(m, post_g_ref[...])


def _mixer(robust, x, pre_g, w_in, tril, w_up, b_gate, norm_g, w_gla, dw_w, dw_b, ln_g, ln_b, w_conv, w_out,
           post_g):
    batch, seq, _ = x.shape
    ts = MIX_ROWS
    assert seq % ts == 0 and ts % GLA_CHUNK == 0 and ts % ROW_BLOCK == 0
    n_seq_tiles = seq // ts
    tok_spec = pl.BlockSpec((None, ts, D_MODEL), lambda b, s: (b, s, 0))
    consts = (pre_g, w_in, tril, w_up, b_gate, norm_g, w_gla, dw_w, dw_b, ln_g, ln_b, w_conv, w_out, post_g)
    out_specs, out_shape = tok_spec, jax.ShapeDtypeStruct(x.shape, _F32)
    scratch = [
        pltpu.VMEM((GLA_HEADS, GLA_HK, GLA_HV), _F32),
        pltpu.VMEM((CONV_TAIL + ts, D_MODEL), _F32),
        pltpu.VMEM((ts, _GLR0 - _Z0), _F32),
        pltpu.VMEM((ts, GLA_DV), _F32),
        pltpu.VMEM((ts, D_MODEL), _F32),
        pltpu.VMEM((ts, D_MODEL), _BF16),
    ]
    if robust:
        scratch += [
            pltpu.VMEM((ts, GLA_DK), _F32),
            pltpu.VMEM((ts // GLA_CHUNK, GLA_HEADS, GLA_CHUNK, GLA_CHUNK), _F32),
            pltpu.VMEM((ts, D_MODEL), _BF16),
            pltpu.VMEM((ts, GLA_HK), _F32),
        ]
    else:
        out_specs = [tok_spec, pl.BlockSpec((None, None, SUBLANES, LANES), lambda b, s: (b, s, 0, 0))]
        out_shape = [out_shape, jax.ShapeDtypeStruct((batch, n_seq_tiles, SUBLANES, LANES), _F32)]
    return pl.pallas_call(
        functools.partial(_mixer_kernel, robust),
        grid=(batch, n_seq_tiles),
        in_specs=[tok_spec] + [_resident(c.shape) for c in consts],
        out_specs=out_specs,
        out_shape=out_shape,
        scratch_shapes=scratch,
        compiler_params=pltpu.CompilerParams(
            dimension_semantics=("arbitrary", "arbitrary"), vmem_limit_bytes=VMEM_LIMIT_BYTES),
        name="gla_conv_mixer_checked" if robust else "gla_conv_mixer",
    )(x, *consts)


def _chunk_tril(n, chunk):
    i = jnp.arange(n)
    same = (i[:, None] // chunk) == (i[None, :] // chunk)
    return (same & (i[None, :] <= i[:, None])).astype(_BF16)


def _mixer_in_weights(w):
    q_end = 2 * GLA_DK + GLA_DV
    lr_end = q_end + GLA_GATE_RANK
    og_end = lr_end + GLA_DV
    cu_end = og_end + 2 * D_MODEL
    w_glr = jnp.pad(w[:, q_end:lr_end], ((0, 0), (0, RANK_PAD - GLA_GATE_RANK)))
    return jnp.concatenate([w[:, og_end:cu_end], w[:, :q_end], w[:, lr_end:og_end], w[:, cu_end:], w_glr],
                           axis=1).astype(_BF16)


def kernel(x, ffn1_pre_g, ffn1_w_in, ffn1_w_out, ffn1_post_g, mix_pre_g, w_mix_in, gla_w_gate_up, gla_b_gate,
           gla_norm_g, gla_w_proj, conv_dw_w, conv_dw_b, conv_ln_g, conv_ln_b, conv_w_proj, w_mix_out,
           mix_post_g, ffn2_pre_g, ffn2_w_in, ffn2_w_out, ffn2_post_g):
    batch, seq, d = x.shape
    depth = ffn1_pre_g.shape[0]
    tril = _chunk_tril(MIX_ROWS, GLA_CHUNK)
    for l in range(depth):
        row = lambda p: p[l].reshape(1, -1)
        x2d = _ffn(x.reshape(batch * seq, d), row(ffn1_pre_g), ffn1_w_in[l].astype(_BF16),
                   ffn1_w_out[l].astype(_BF16), row(ffn1_post_g))
        w_up = jnp.pad(gla_w_gate_up[l], ((0, RANK_PAD - GLA_GATE_RANK), (0, 0))).astype(_BF16)
        x1 = x2d.reshape(batch, seq, d)
        mixer_args = (row(mix_pre_g), _mixer_in_weights(w_mix_in[l]), tril, w_up, row(gla_b_gate),
                      row(gla_norm_g), gla_w_proj[l].astype(_BF16), conv_dw_w[l], row(conv_dw_b),
                      row(conv_ln_g), row(conv_ln_b), conv_w_proj[l].astype(_BF16), w_mix_out[l].astype(_BF16),
                      row(mix_post_g))
        ffn2_args = (row(ffn2_pre_g), ffn2_w_in[l].astype(_BF16), ffn2_w_out[l].astype(_BF16), row(ffn2_post_g))
        x2, min_decay = _mixer(False, x1, *mixer_args)
        x2d = lax.cond(
            jnp.min(min_decay) >= -GLA_FACTOR_LOG_RANGE,
            lambda x1, x2: _ffn(x2.reshape(batch * seq, d), *ffn2_args),
            lambda x1, x2: _ffn(_mixer(True, x1, *mixer_args).reshape(batch * seq, d), *ffn2_args),
            x1, x2)
        x = x2d.reshape(batch, seq, d)
    return x
```

```python
import functools
import itertools

import jax
import jax.numpy as jnp
from jax import lax
from jax.experimental import pallas as pl
from jax.experimental.pallas import tpu as pltpu

D_MODEL = 1024
D_FF = 2816
GLA_HEADS = 4
GLA_HK = 128
GLA_HV = 256
GLA_DK = GLA_HEADS * GLA_HK
GLA_DV = GLA_HEADS * GLA_HV
GLA_GATE_RANK = 16
GLA_GATE_NORMALIZER = 16.0
CONV_WIDTH = 31
EPS = 1e-6

SUBLANES = 8
LANES = 128
MXU_TILE = 256
VMEM_LIMIT_BYTES = 56 * 1024 * 1024

FFN_ROWS = 512
MIX_ROWS = 512
GLA_CHUNK = 128
ROW_BLOCK = 128
CONV_TAIL = 32
CONV_LANE_BLOCK = 128
RANK_PAD = LANES
GLA_FACTOR_LOG_RANGE = 60.0

_UA0 = 0
_UB0 = _UA0 + D_MODEL
_Z0 = _UB0 + D_MODEL
_Q0 = _Z0
_K0 = _Q0 + GLA_DK
_V0 = _K0 + GLA_DK
_OG0 = _V0 + GLA_DV
_GA0 = _OG0 + GLA_DV
_GB0 = _GA0 + D_MODEL
_GLR0 = _GB0 + D_MODEL
MIX_IN_PAD = _GLR0 + RANK_PAD

_F32 = jnp.float32
_BF16 = jnp.bfloat16


def _dot(a, b):
    return jnp.dot(a, b, preferred_element_type=_F32)


def _rms(x, g):
    return x * lax.rsqrt(jnp.mean(x * x, axis=-1, keepdims=True) + EPS) * g


def _row_mean_mxu(v):
    n = v.shape[-1]
    part = functools.reduce(jnp.add, [v[:, c:c + LANES] for c in range(0, n, LANES)])
    hi = part.astype(_BF16)
    lo = (part - hi.astype(_F32)).astype(_BF16)
    ones = jnp.ones((LANES, LANES), _BF16)
    mean = (_dot(hi, ones) + _dot(lo, ones)) * (1.0 / n)
    return jnp.concatenate([mean] * (n // LANES), axis=1)


def _rms_mxu(x, g):
    return x * lax.rsqrt(_row_mean_mxu(x * x) + EPS) * g


def _interleave(*task_lists):
    longest = max(len(t) for t in task_lists)
    done = [0] * len(task_lists)
    for step in range(1, longest + 1):
        for i, tasks in enumerate(task_lists):
            upto = (step * len(tasks) + longest - 1) // longest
            for task in tasks[done[i]:upto]:
                task()
            done[i] = upto


def _ffn_kernel(x_ref, pre_g_ref, w_in_ref, w_out_ref, post_g_ref, o_ref):
    x = x_ref[...]
    r = lax.rsqrt(jnp.mean(x * x, axis=-1, keepdims=True) + EPS)
    xg = (x * pre_g_ref[...]).astype(_BF16)
    gate = _dot(xg, w_in_ref[:, :D_FF]) * r
    up = _dot(xg, w_in_ref[:, D_FF:]) * r
    act = (jax.nn.silu(gate) * up).astype(_BF16)
    f = _dot(act, w_out_ref[...])
    o_ref[...] = x + _rms(f, 0.5 * post_g_ref[...])


def _resident(shape):
    return pl.BlockSpec(shape, lambda *_: (0,) * len(shape), pipeline_mode=pl.Buffered(1))


def _ffn(x2d, pre_g, w_in, w_out, post_g):
    n_tok = x2d.shape[0]
    assert n_tok % FFN_ROWS == 0
    row_spec = pl.BlockSpec((FFN_ROWS, D_MODEL), lambda i: (i, 0))
    return pl.pallas_call(
        _ffn_kernel,
        grid=(n_tok // FFN_ROWS,),
        in_specs=[row_spec, _resident((1, D_MODEL)), _resident((D_MODEL, 2 * D_FF)),
                  _resident((D_FF, D_MODEL)), _resident((1, D_MODEL))],
        out_specs=row_spec,
        out_shape=jax.ShapeDtypeStruct(x2d.shape, _F32),
        compiler_params=pltpu.CompilerParams(
            dimension_semantics=("arbitrary",), vmem_limit_bytes=VMEM_LIMIT_BYTES),
        name="ffn_half_step",
    )(x2d, pre_g, w_in, w_out, post_g)


def _mixer_kernel(robust, x_ref, pre_g_ref, w_in_ref, tril_ref, w_up_ref, b_gate_ref, norm_g_ref, w_gla_ref,
                  dw_w_ref, dw_b_ref, ln_g_ref, ln_b_ref, w_conv_ref, w_out_ref, post_g_ref, o_ref, *rest):
    if robust:
        state_ref, ubuf_ref, zbuf_ref, obuf_ref, ybuf_ref, cbuf_ref, ball_ref, attn_ref, h_ref, kbuf_ref = rest
    else:
        decay_ref, state_ref, ubuf_ref, zbuf_ref, obuf_ref, ybuf_ref, cbuf_ref = rest
    ts = MIX_ROWS
    n_chunks = ts // GLA_CHUNK
    row_blocks = [slice(r, r + ROW_BLOCK) for r in range(0, ts, ROW_BLOCK)]

    def zcols(c0, c1):
        return slice(c0 - _Z0, c1 - _Z0)

    @pl.when(pl.program_id(1) == 0)
    def _():
        state_ref[...] = jnp.zeros_like(state_ref)
        ubuf_ref[0:CONV_TAIL, :] = jnp.zeros((CONV_TAIL, D_MODEL), _F32)
        if robust:
            attn_ref[...] = jnp.zeros_like(attn_ref)

    h = _rms_mxu(x_ref[...], pre_g_ref[...]).astype(_BF16)

    def cumulative_log_decay():
        g_lr = _dot(h, w_in_ref[:, _GLR0:MIX_IN_PAD]).astype(_BF16)
        logit = _dot(g_lr, w_up_ref[...]) + b_gate_ref[...]
        log_a = jax.nn.log_sigmoid(logit) / GLA_GATE_NORMALIZER
        la_hi = log_a.astype(_BF16)
        la_lo = (log_a - la_hi.astype(_F32)).astype(_BF16)
        tril = tril_ref[...]
        return _dot(tril, la_hi) + _dot(tril, la_lo)

    if robust:
        h_ref[...] = h
        ball_ref[...] = cumulative_log_decay()
        decay_in_range = jnp.min(ball_ref[...]) >= -GLA_FACTOR_LOG_RANGE

        @pl.when(jnp.logical_not(decay_in_range))
        def _():
            col_id = lax.broadcasted_iota(jnp.int32, (GLA_CHUNK, GLA_CHUNK), 1)
            for hd in range(GLA_HEADS):
                ks = slice(hd * GLA_HK, (hd + 1) * GLA_HK)
                kbuf_ref[...] = _dot(h_ref[...], w_in_ref[:, _K0 + ks.start:_K0 + ks.stop])
                for c in range(n_chunks):
                    rows = slice(c * GLA_CHUNK, (c + 1) * GLA_CHUNK)
                    q_c = _dot(h_ref[rows, :], w_in_ref[:, _Q0 + ks.start:_Q0 + ks.stop]) * (GLA_HK ** -0.5)
                    b_c = ball_ref[rows, ks]

                    def columns(jb, acc, rows=rows, ks=ks, q_c=q_c, b_c=b_c):
                        src = pl.ds(pl.multiple_of(rows.start + jb * SUBLANES, SUBLANES), SUBLANES)
                        b_j, k_j = ball_ref[src, ks], kbuf_ref[src, :]
                        for r in range(SUBLANES):
                            w = jnp.exp(jnp.minimum(b_c - b_j[r:r + 1], 0.0))
                            a_j = jnp.sum(q_c * k_j[r:r + 1] * w, axis=-1, keepdims=True)
                            acc = acc + jnp.where(col_id == jb * SUBLANES + r, a_j, 0.0)
                        return acc

                    attn_ref[c, hd] = lax.fori_loop(0, GLA_CHUNK // SUBLANES, columns,
                                                    jnp.zeros((GLA_CHUNK, GLA_CHUNK), _F32))

        h = h_ref[...]
        b_all = ball_ref[...]

    ua = _dot(h, w_in_ref[:, _UA0:_UB0])
    ub = _dot(h, w_in_ref[:, _UB0:_Z0])
    ubuf_ref[CONV_TAIL:CONV_TAIL + ts, :] = ua * jax.nn.sigmoid(ub)

    tap0 = CONV_TAIL - (CONV_WIDTH - 1)
    win = ROW_BLOCK + CONV_TAIL

    def conv_block(rows, cs):
        xw = ubuf_ref[rows.start:rows.start + win, cs]
        acc = jnp.broadcast_to(dw_b_ref[:, cs], (ROW_BLOCK, CONV_LANE_BLOCK))
        for r in range(SUBLANES):
            xr = xw if r == 0 else pltpu.roll(xw, win - r, axis=0)
            for a in range(win // SUBLANES):
                tau = SUBLANES * a + r - tap0
                if 0 <= tau < CONV_WIDTH:
                    acc = acc + xr[SUBLANES * a:SUBLANES * a + ROW_BLOCK] * dw_w_ref[tau:tau + 1, cs]
        ybuf_ref[rows, cs] = acc

    def proj_piece(c0):
        zbuf_ref[:, zcols(c0, c0 + MXU_TILE)] = _dot(h, w_in_ref[:, c0:c0 + MXU_TILE])

    conv_tasks = [lambda rows=rows, c=c: conv_block(rows, slice(c, c + CONV_LANE_BLOCK))
                  for rows in row_blocks for c in range(0, D_MODEL, CONV_LANE_BLOCK)]
    proj_tasks = [lambda c=c: proj_piece(c) for c in range(_Z0, _GLR0, MXU_TILE)]
    _interleave(conv_tasks, proj_tasks)
    ubuf_ref[0:CONV_TAIL, :] = ubuf_ref[ts:ts + CONV_TAIL, :]

    if not robust:
        b_all = cumulative_log_decay()
        tile_min = jnp.min(b_all, axis=0, keepdims=True)
        tile_min = functools.reduce(jnp.minimum, [tile_min[:, c:c + LANES] for c in range(0, GLA_DK, LANES)])
        decay_ref[...] = jnp.broadcast_to(tile_min, (SUBLANES, LANES))

    def chunk_slices(c, hd):
        rows = slice(c * GLA_CHUNK, (c + 1) * GLA_CHUNK)
        ks = slice(hd * GLA_HK, (hd + 1) * GLA_HK)
        vs = slice(hd * GLA_HV, (hd + 1) * GLA_HV)
        return (rows, ks, vs, zcols(_Q0 + ks.start, _Q0 + ks.stop), zcols(_K0 + ks.start, _K0 + ks.stop),
                zcols(_V0 + vs.start, _V0 + vs.stop))

    row = lax.broadcasted_iota(jnp.int32, (GLA_CHUNK, GLA_CHUNK), 0)
    col = lax.broadcasted_iota(jnp.int32, (GLA_CHUNK, GLA_CHUNK), 1)
    causal = col <= row
    mid = GLA_CHUNK // 2

    def gla_chunk_head(c, hd):
        rows, ks, vs, qc, kc, vc = chunk_slices(c, hd)
        b = b_all[rows, ks]
        b_mid = b[mid - 1:mid]
        b_last = b[GLA_CHUNK - 1:GLA_CHUNK]
        q_c = zbuf_ref[rows, qc] * (GLA_HK ** -0.5)
        k_c = zbuf_ref[rows, kc]
        v_h = zbuf_ref[rows, vc].astype(_BF16)
        e = b - b_mid
        qd = (q_c * jnp.exp(e)).astype(_BF16)
        kd = (k_c * jnp.exp(-e)).astype(_BF16)
        qb = (q_c * jnp.exp(b)).astype(_BF16)
        kl = (k_c * jnp.exp(b_last - b)).astype(_BF16)
        attn = lax.dot_general(qd, kd, (((1,), (1,)), ((), ())), preferred_element_type=_F32)
        if robust:
            attn = jnp.where(decay_in_range, attn, attn_ref[c, hd])
        attn = jnp.where(causal, attn, 0.0).astype(_BF16)
        s_h = state_ref[hd]
        obuf_ref[rows, vs] = _dot(attn, v_h) + _dot(qb, s_h.astype(_BF16))
        a_col = jnp.broadcast_to(jnp.exp(b_last), (GLA_HK, GLA_HK)).T
        a_col = jnp.concatenate([a_col] * (GLA_HV // GLA_HK), axis=1)
        state_ref[hd] = s_h * a_col + lax.dot_general(
            kl, v_h, (((0,), (0,)), ((), ())), preferred_element_type=_F32)

    def ln_silu(rows):
        yc = ybuf_ref[rows, :]
        yc0 = yc - _row_mean_mxu(yc)
        var = _row_mean_mxu(yc0 * yc0)
        ln = yc0 * lax.rsqrt(var + EPS) * ln_g_ref[...] + ln_b_ref[...]
        cbuf_ref[rows, :] = jax.nn.silu(ln).astype(_BF16)

    def in_place(fn, rows, c0):
        cols = zcols(c0, c0 + D_MODEL)
        zbuf_ref[rows, cols] = fn(zbuf_ref[rows, cols])

    gla_tasks = [lambda c=c, hd=hd: gla_chunk_head(c, hd) for c in range(n_chunks) for hd in range(GLA_HEADS)]
    vpu_tasks = list(itertools.chain.from_iterable(
        (lambda rows=rows: ln_silu(rows),
         lambda rows=rows: in_place(jax.nn.silu, rows, _OG0),
         lambda rows=rows: in_place(jax.nn.sigmoid, rows, _GA0),
         lambda rows=rows: in_place(jax.nn.sigmoid, rows, _GB0)) for rows in row_blocks))
    _interleave(gla_tasks, vpu_tasks)

    y_conv = _dot(cbuf_ref[...], w_conv_ref[...])
    norm_g = norm_g_ref[...]
    heads = []
    for hd in range(GLA_HEADS):
        vs = slice(hd * GLA_HV, (hd + 1) * GLA_HV)
        sil_og = zbuf_ref[:, zcols(_OG0 + vs.start, _OG0 + vs.stop)]
        heads.append((_rms_mxu(obuf_ref[:, vs], norm_g) * sil_og).astype(_BF16))
    y_gla = _dot(jnp.concatenate(heads, axis=1), w_gla_ref[...])

    gate_a = zbuf_ref[:, zcols(_GA0, _GB0)]
    gate_b = zbuf_ref[:, zcols(_GB0, _GLR0)]
    m = _dot((gate_a * y_gla + gate_b * y_conv).astype(_BF16), w_out_ref[...])
    o_ref[...] = x_ref[...] + _rms_mxu(m, post_g_ref[...])


def _mixer(robust, x, pre_g, w_in, tril, w_up, b_gate, norm_g, w_gla, dw_w, dw_b, ln_g, ln_b, w_conv, w_out,
           post_g):
    batch, seq, _ = x.shape
    ts = MIX_ROWS
    assert seq % ts == 0 and ts % GLA_CHUNK == 0 and ts % ROW_BLOCK == 0
    n_seq_tiles = seq // ts
    tok_spec = pl.BlockSpec((None, ts, D_MODEL), lambda b, s: (b, s, 0))
    consts = (pre_g, w_in, tril, w_up, b_gate, norm_g, w_gla, dw_w, dw_b, ln_g, ln_b, w_conv, w_out, post_g)
    out_specs, out_shape = tok_spec, jax.ShapeDtypeStruct(x.shape, _F32)
    scratch = [
        pltpu.VMEM((GLA_HEADS, GLA_HK, GLA_HV), _F32),
        pltpu.VMEM((CONV_TAIL + ts, D_MODEL), _F32),
        pltpu.VMEM((ts, _GLR0 - _Z0), _F32),
        pltpu.VMEM((ts, GLA_DV), _F32),
        pltpu.VMEM((ts, D_MODEL), _F32),
        pltpu.VMEM((ts, D_MODEL), _BF16),
    ]
    if robust:
        scratch += [
            pltpu.VMEM((ts, GLA_DK), _F32),
            pltpu.VMEM((ts // GLA_CHUNK, GLA_HEADS, GLA_CHUNK, GLA_CHUNK), _F32),
            pltpu.VMEM((ts, D_MODEL), _BF16),
            pltpu.VMEM((ts, GLA_HK), _F32),
        ]
    else:
        out_specs = [tok_spec, pl.BlockSpec((None, None, SUBLANES, LANES), lambda b, s: (b, s, 0, 0))]
        out_shape = [out_shape, jax.ShapeDtypeStruct((batch, n_seq_tiles, SUBLANES, LANES), _F32)]
    return pl.pallas_call(
        functools.partial(_mixer_kernel, robust),
        grid=(batch, n_seq_tiles),
        in_specs=[tok_spec] + [_resident(c.shape) for c in consts],
        out_specs=out_specs,
        out_shape=out_shape,
        scratch_shapes=scratch,
        compiler_params=pltpu.CompilerParams(
            dimension_semantics=("arbitrary", "arbitrary"), vmem_limit_bytes=VMEM_LIMIT_BYTES),
        name="gla_conv_mixer_checked" if robust else "gla_conv_mixer",
    )(x, *consts)


def _chunk_tril(n, chunk):
    i = jnp.arange(n)
    same = (i[:, None] // chunk) == (i[None, :] // chunk)
    return (same & (i[None, :] <= i[:, None])).astype(_BF16)


def _mixer_in_weights(w):
    q_end = 2 * GLA_DK + GLA_DV
    lr_end = q_end + GLA_GATE_RANK
    og_end = lr_end + GLA_DV
    cu_end = og_end + 2 * D_MODEL
    w_glr = jnp.pad(w[:, q_end:lr_end], ((0, 0), (0, RANK_PAD - GLA_GATE_RANK)))
    return jnp.concatenate([w[:, og_end:cu_end], w[:, :q_end], w[:, lr_end:og_end], w[:, cu_end:], w_glr],
                           axis=1).astype(_BF16)


def kernel(x, ffn1_pre_g, ffn1_w_in, ffn1_w_out, ffn1_post_g, mix_pre_g, w_mix_in, gla_w_gate_up, gla_b_gate,
           gla_norm_g, gla_w_proj, conv_dw_w, conv_dw_b, conv_ln_g, conv_ln_b, conv_w_proj, w_mix_out,
           mix_post_g, ffn2_pre_g, ffn2_w_in, ffn2_w_out, ffn2_post_g):
    batch, seq, d = x.shape
    depth = ffn1_pre_g.shape[0]
    tril = _chunk_tril(MIX_ROWS, GLA_CHUNK)
    for l in range(depth):
        row = lambda p: p[l].reshape(1, -1)
        x2d = _ffn(x.reshape(batch * seq, d), row(ffn1_pre_g), ffn1_w_in[l].astype(_BF16),
                   ffn1_w_out[l].astype(_BF16), row(ffn1_post_g))
        w_up = jnp.pad(gla_w_gate_up[l], ((0, RANK_PAD - GLA_GATE_RANK), (0, 0))).astype(_BF16)
        x1 = x2d.reshape(batch, seq, d)
        mixer_args = (row(mix_pre_g), _mixer_in_weights(w_mix_in[l]), tril, w_up, row(gla_b_gate),
                      row(gla_norm_g), gla_w_proj[l].astype(_BF16), conv_dw_w[l], row(conv_dw_b),
                      row(conv_ln_g), row(conv_ln_b), conv_w_proj[l].astype(_BF16), w_mix_out[l].astype(_BF16),
                      row(mix_post_g))
        ffn2_args = (row(ffn2_pre_g), ffn2_w_in[l].astype(_BF16), ffn2_w_out[l].astype(_BF16), row(ffn2_post_g))
        x2, min_decay = _mixer(False, x1, *mixer_args)
        x2d = lax.cond(
            jnp.min(min_decay) >= -GLA_FACTOR_LOG_RANGE,
            lambda x1, x2: _ffn(x2.reshape(batch * seq, d), *ffn2_args),
            lambda x1, x2: _ffn(_mixer(True, x1, *mixer_args).reshape(batch * seq, d), *ffn2_args),
            x1, x2)
        x = x2d.reshape(batch, seq, d)
    return x
```

```python
import functools
import itertools

import jax
import jax.numpy as jnp
from jax import lax
from jax.experimental import pallas as pl
from jax.experimental.pallas import tpu as pltpu

D_MODEL = 1024
D_FF = 2816
GLA_HEADS = 4
GLA_HK = 128
GLA_HV = 256
GLA_DK = GLA_HEADS * GLA_HK
GLA_DV = GLA_HEADS * GLA_HV
GLA_GATE_RANK = 16
GLA_GATE_NORMALIZER = 16.0
CONV_WIDTH = 31
EPS = 1e-6

SUBLANES = 8
LANES = 128
MXU_TILE = 256
VMEM_LIMIT_BYTES = 56 * 1024 * 1024

FFN_ROWS = 512
MIX_ROWS = 512
GLA_CHUNK = 128
ROW_BLOCK = 128
CONV_TAIL = 32
CONV_LANE_BLOCK = 128
RANK_PAD = LANES
GLA_FACTOR_LOG_RANGE = 60.0

_UA0 = 0
_UB0 = _UA0 + D_MODEL
_Z0 = _UB0 + D_MODEL
_Q0 = _Z0
_K0 = _Q0 + GLA_DK
_V0 = _K0 + GLA_DK
_OG0 = _V0 + GLA_DV
_GA0 = _OG0 + GLA_DV
_GB0 = _GA0 + D_MODEL
_GLR0 = _GB0 + D_MODEL
MIX_IN_PAD = _GLR0 + RANK_PAD

_F32 = jnp.float32
_BF16 = jnp.bfloat16


def _dot(a, b):
    return jnp.dot(a, b, preferred_element_type=_F32)


def _rms(x, g):
    return x * lax.rsqrt(jnp.mean(x * x, axis=-1, keepdims=True) + EPS) * g


def _row_mean_mxu(v):
    n = v.shape[-1]
    part = functools.reduce(jnp.add, [v[:, c:c + LANES] for c in range(0, n, LANES)])
    hi = part.astype(_BF16)
    lo = (part - hi.astype(_F32)).astype(_BF16)
    ones = jnp.ones((2 * LANES, LANES), _BF16)
    mean = _dot(jnp.concatenate([hi, lo], axis=1), ones) * (1.0 / n)
    return jnp.concatenate([mean] * (n // LANES), axis=1)


def _rms_mxu(x, g):
    return x * lax.rsqrt(_row_mean_mxu(x * x) + EPS) * g


def _interleave(*task_lists):
    longest = max(len(t) for t in task_lists)
    done = [0] * len(task_lists)
    for step in range(1, longest + 1):
        for i, tasks in enumerate(task_lists):
            upto = (step * len(tasks) + longest - 1) // longest
            for task in tasks[done[i]:upto]:
                task()
            done[i] = upto


def _ffn_kernel(x_ref, pre_g_ref, w_in_ref, w_out_ref, post_g_ref, o_ref):
    x = x_ref[...]
    r = lax.rsqrt(jnp.mean(x * x, axis=-1, keepdims=True) + EPS)
    xg = (x * pre_g_ref[...]).astype(_BF16)
    gate = _dot(xg, w_in_ref[:, :D_FF]) * r
    up = _dot(xg, w_in_ref[:, D_FF:]) * r
    act = (jax.nn.silu(gate) * up).astype(_BF16)
    f = _dot(act, w_out_ref[...])
    o_ref[...] = x + _rms(f, 0.5 * post_g_ref[...])


def _resident(shape):
    return pl.BlockSpec(shape, lambda *_: (0,) * len(shape), pipeline_mode=pl.Buffered(1))


def _ffn(x2d, pre_g, w_in, w_out, post_g):
    n_tok = x2d.shape[0]
    assert n_tok % FFN_ROWS == 0
    row_spec = pl.BlockSpec((FFN_ROWS, D_MODEL), lambda i: (i, 0))
    return pl.pallas_call(
        _ffn_kernel,
        grid=(n_tok // FFN_ROWS,),
        in_specs=[row_spec, _resident((1, D_MODEL)), _resident((D_MODEL, 2 * D_FF)),
                  _resident((D_FF, D_MODEL)), _resident((1, D_MODEL))],
        out_specs=row_spec,
        out_shape=jax.ShapeDtypeStruct(x2d.shape, _F32),
        compiler_params=pltpu.CompilerParams(
            dimension_semantics=("arbitrary",), vmem_limit_bytes=VMEM_LIMIT_BYTES),
        name="ffn_half_step",
    )(x2d, pre_g, w_in, w_out, post_g)


def _mixer_kernel(robust, x_ref, pre_g_ref, w_in_ref, tril_ref, w_up_ref, b_gate_ref, norm_g_ref, w_gla_ref,
                  dw_w_ref, dw_b_ref, ln_g_ref, ln_b_ref, w_conv_ref, w_out_ref, post_g_ref, o_ref, *rest):
    if robust:
        state_ref, ubuf_ref, zbuf_ref, obuf_ref, ybuf_ref, cbuf_ref, ball_ref, attn_ref, h_ref, kbuf_ref = rest
    else:
        decay_ref, state_ref, ubuf_ref, zbuf_ref, obuf_ref, ybuf_ref, cbuf_ref = rest
    ts = MIX_ROWS
    n_chunks = ts // GLA_CHUNK
    row_blocks = [slice(r, r + ROW_BLOCK) for r in range(0, ts, ROW_BLOCK)]

    def zcols(c0, c1):
        return slice(c0 - _Z0, c1 - _Z0)

    @pl.when(pl.program_id(1) == 0)
    def _():
        state_ref[...] = jnp.zeros_like(state_ref)
        ubuf_ref[0:CONV_TAIL, :] = jnp.zeros((CONV_TAIL, D_MODEL), _F32)
        if robust:
            attn_ref[...] = jnp.zeros_like(attn_ref)

    h = _rms_mxu(x_ref[...], pre_g_ref[...]).astype(_BF16)

    def cumulative_log_decay():
        g_lr = _dot(h, w_in_ref[:, _GLR0:MIX_IN_PAD]).astype(_BF16)
        logit = _dot(g_lr, w_up_ref[...]) + b_gate_ref[...]
        log_a = jax.nn.log_sigmoid(logit) / GLA_GATE_NORMALIZER
        la_hi = log_a.astype(_BF16)
        la_lo = (log_a - la_hi.astype(_F32)).astype(_BF16)
        tril = tril_ref[...]
        return _dot(tril, la_hi) + _dot(tril, la_lo)

    if robust:
        h_ref[...] = h
        ball_ref[...] = cumulative_log_decay()
        decay_in_range = jnp.min(ball_ref[...]) >= -GLA_FACTOR_LOG_RANGE

        @pl.when(jnp.logical_not(decay_in_range))
        def _():
            col_id = lax.broadcasted_iota(jnp.int32, (GLA_CHUNK, GLA_CHUNK), 1)
            for hd in range(GLA_HEADS):
                ks = slice(hd * GLA_HK, (hd + 1) * GLA_HK)
                kbuf_ref[...] = _dot(h_ref[...], w_in_ref[:, _K0 + ks.start:_K0 + ks.stop])
                for c in range(n_chunks):
                    rows = slice(c * GLA_CHUNK, (c + 1) * GLA_CHUNK)
                    q_c = _dot(h_ref[rows, :], w_in_ref[:, _Q0 + ks.start:_Q0 + ks.stop]) * (GLA_HK ** -0.5)
                    b_c = ball_ref[rows, ks]

                    def columns(jb, acc, rows=rows, ks=ks, q_c=q_c, b_c=b_c):
                        src = pl.ds(pl.multiple_of(rows.start + jb * SUBLANES, SUBLANES), SUBLANES)
                        b_j, k_j = ball_ref[src, ks], kbuf_ref[src, :]
                        for r in range(SUBLANES):
                            w = jnp.exp(jnp.minimum(b_c - b_j[r:r + 1], 0.0))
                            a_j = jnp.sum(q_c * k_j[r:r + 1] * w, axis=-1, keepdims=True)
                            acc = acc + jnp.where(col_id == jb * SUBLANES + r, a_j, 0.0)
                        return acc

                    attn_ref[c, hd] = lax.fori_loop(0, GLA_CHUNK // SUBLANES, columns,
                                                    jnp.zeros((GLA_CHUNK, GLA_CHUNK), _F32))

        h = h_ref[...]
        b_all = ball_ref[...]

    ua = _dot(h, w_in_ref[:, _UA0:_UB0])
    ub = _dot(h, w_in_ref[:, _UB0:_Z0])
    ubuf_ref[CONV_TAIL:CONV_TAIL + ts, :] = ua * jax.nn.sigmoid(ub)

    tap0 = CONV_TAIL - (CONV_WIDTH - 1)
    win = ROW_BLOCK + CONV_TAIL

    def conv_block(rows, cs):
        xw = ubuf_ref[rows.start:rows.start + win, cs]
        acc = jnp.broadcast_to(dw_b_ref[:, cs], (ROW_BLOCK, CONV_LANE_BLOCK))
        for r in range(SUBLANES):
            xr = xw if r == 0 else pltpu.roll(xw, win - r, axis=0)
            for a in range(win // SUBLANES):
                tau = SUBLANES * a + r - tap0
                if 0 <= tau < CONV_WIDTH:
                    acc = acc + xr[SUBLANES * a:SUBLANES * a + ROW_BLOCK] * dw_w_ref[tau:tau + 1, cs]
        ybuf_ref[rows, cs] = acc

    def proj_piece(c0):
        zbuf_ref[:, zcols(c0, c0 + MXU_TILE)] = _dot(h, w_in_ref[:, c0:c0 + MXU_TILE])

    conv_tasks = [lambda rows=rows, c=c: conv_block(rows, slice(c, c + CONV_LANE_BLOCK))
                  for rows in row_blocks for c in range(0, D_MODEL, CONV_LANE_BLOCK)]
    proj_tasks = [lambda c=c: proj_piece(c) for c in range(_Z0, _GLR0, MXU_TILE)]
    _interleave(conv_tasks, proj_tasks)
    ubuf_ref[0:CONV_TAIL, :] = ubuf_ref[ts:ts + CONV_TAIL, :]

    if not robust:
        b_all = cumulative_log_decay()
        tile_min = jnp.min(b_all, axis=0, keepdims=True)
        tile_min = functools.reduce(jnp.minimum, [tile_min[:, c:c + LANES] for c in range(0, GLA_DK, LANES)])
        decay_ref[...] = jnp.broadcast_to(tile_min, (SUBLANES, LANES))

    def chunk_slices(c, hd):
        rows = slice(c * GLA_CHUNK, (c + 1) * GLA_CHUNK)
        ks = slice(hd * GLA_HK, (hd + 1) * GLA_HK)
        vs = slice(hd * GLA_HV, (hd + 1) * GLA_HV)
        return (rows, ks, vs, zcols(_Q0 + ks.start, _Q0 + ks.stop), zcols(_K0 + ks.start, _K0 + ks.stop),
                zcols(_V0 + vs.start, _V0 + vs.stop))

    row = lax.broadcasted_iota(jnp.int32, (GLA_CHUNK, GLA_CHUNK), 0)
    col = lax.broadcasted_iota(jnp.int32, (GLA_CHUNK, GLA_CHUNK), 1)
    causal = col <= row
    mid = GLA_CHUNK // 2

    def gla_chunk_head(c, hd):
        rows, ks, vs, qc, kc, vc = chunk_slices(c, hd)
        b = b_all[rows, ks]
        b_mid = b[mid - 1:mid]
        b_last = b[GLA_CHUNK - 1:GLA_CHUNK]
        q_c = zbuf_ref[rows, qc] * (GLA_HK ** -0.5)
        k_c = zbuf_ref[rows, kc]
        v_h = zbuf_ref[rows, vc].astype(_BF16)
        e = b - b_mid
        qd = (q_c * jnp.exp(e)).astype(_BF16)
        kd = (k_c * jnp.exp(-e)).astype(_BF16)
        qb = (q_c * jnp.exp(b)).astype(_BF16)
        kl = (k_c * jnp.exp(b_last - b)).astype(_BF16)
        attn = lax.dot_general(qd, kd, (((1,), (1,)), ((), ())), preferred_element_type=_F32)
        if robust:
            attn = jnp.where(decay_in_range, attn, attn_ref[c, hd])
        attn = jnp.where(causal, attn, 0.0).astype(_BF16)
        s_h = state_ref[hd]
        obuf_ref[rows, vs] = _dot(attn, v_h) + _dot(qb, s_h.astype(_BF16))
        a_col = jnp.broadcast_to(jnp.exp(b_last), (GLA_HK, GLA_HK)).T
        a_col = jnp.concatenate([a_col] * (GLA_HV // GLA_HK), axis=1)
        state_ref[hd] = s_h * a_col + lax.dot_general(
            kl, v_h, (((0,), (0,)), ((), ())), preferred_element_type=_F32)

    def ln_silu(rows):
        yc = ybuf_ref[rows, :]
        yc0 = yc - _row_mean_mxu(yc)
        var = _row_mean_mxu(yc0 * yc0)
        ln = yc0 * lax.rsqrt(var + EPS) * ln_g_ref[...] + ln_b_ref[...]
        cbuf_ref[rows, :] = jax.nn.silu(ln).astype(_BF16)

    def in_place(fn, rows, c0):
        cols = zcols(c0, c0 + D_MODEL)
        zbuf_ref[rows, cols] = fn(zbuf_ref[rows, cols])

    gla_tasks = [lambda c=c, hd=hd: gla_chunk_head(c, hd) for c in range(n_chunks) for hd in range(GLA_HEADS)]
    vpu_tasks = list(itertools.chain.from_iterable(
        (lambda rows=rows: ln_silu(rows),
         lambda rows=rows: in_place(jax.nn.silu, rows, _OG0),
         lambda rows=rows: in_place(jax.nn.sigmoid, rows, _GA0),
         lambda rows=rows: in_place(jax.nn.sigmoid, rows, _GB0)) for rows in row_blocks))
    _interleave(gla_tasks, vpu_tasks)

    y_conv = _dot(cbuf_ref[...], w_conv_ref[...])
    norm_g = norm_g_ref[...]
    heads = []
    for hd in range(GLA_HEADS):
        vs = slice(hd * GLA_HV, (hd + 1) * GLA_HV)
        sil_og = zbuf_ref[:, zcols(_OG0 + vs.start, _OG0 + vs.stop)]
        heads.append((_rms_mxu(obuf_ref[:, vs], norm_g) * sil_og).astype(_BF16))
    y_gla = _dot(jnp.concatenate(heads, axis=1), w_gla_ref[...])

    gate_a = zbuf_ref[:, zcols(_GA0, _GB0)]
    gate_b = zbuf_ref[:, zcols(_GB0, _GLR0)]
    m = _dot((gate_a * y_gla + gate_b * y_conv).astype(_BF16), w_out_ref[...])
    o_ref[...] = x_ref[...] + _rms_mxu(m, post_g_ref[...])


def _mixer(robust, x, pre_g, w_in, tril, w_up, b_gate, norm_g, w_gla, dw_w, dw_b, ln_g, ln_b, w_conv, w_out,
           post_g):
    batch, seq, _ = x.shape
    ts = MIX_ROWS
    assert seq % ts == 0 and ts % GLA_CHUNK == 0 and ts % ROW_BLOCK == 0
    n_seq_tiles = seq // ts
    tok_spec = pl.BlockSpec((None, ts, D_MODEL), lambda b, s: (b, s, 0))
    consts = (pre_g, w_in, tril, w_up, b_gate, norm_g, w_gla, dw_w, dw_b, ln_g, ln_b, w_conv, w_out, post_g)
    out_specs, out_shape = tok_spec, jax.ShapeDtypeStruct(x.shape, _F32)
    scratch = [
        pltpu.VMEM((GLA_HEADS, GLA_HK, GLA_HV), _F32),
        pltpu.VMEM((CONV_TAIL + ts, D_MODEL), _F32),
        pltpu.VMEM((ts, _GLR0 - _Z0), _F32),
        pltpu.VMEM((ts, GLA_DV), _F32),
        pltpu.VMEM((ts, D_MODEL), _F32),
        pltpu.VMEM((ts, D_MODEL), _BF16),
    ]
    if robust:
        scratch += [
            pltpu.VMEM((ts, GLA_DK), _F32),
            pltpu.VMEM((ts // GLA_CHUNK, GLA_HEADS, GLA_CHUNK, GLA_CHUNK), _F32),
            pltpu.VMEM((ts, D_MODEL), _BF16),
            pltpu.VMEM((ts, GLA_HK), _F32),
        ]
    else:
        out_specs = [tok_spec, pl.BlockSpec((None, None, SUBLANES, LANES), lambda b, s: (b, s, 0, 0))]
        out_shape = [out_shape, jax.ShapeDtypeStruct((batch, n_seq_tiles, SUBLANES, LANES), _F32)]
    return pl.pallas_call(
        functools.partial(_mixer_kernel, robust),
        grid=(batch, n_seq_tiles),
        in_specs=[tok_spec] + [_resident(c.shape) for c in consts],
        out_specs=out_specs,
        out_shape=out_shape,
        scratch_shapes=scratch,
        compiler_params=pltpu.CompilerParams(
            dimension_semantics=("arbitrary", "arbitrary"), vmem_limit_bytes=VMEM_LIMIT_BYTES),
        name="gla_conv_mixer_checked" if robust else "gla_conv_mixer",
    )(x, *consts)


def _chunk_tril(n, chunk):
    i = jnp.arange(n)
    same = (i[:, None] // chunk) == (i[None, :] // chunk)
    return (same & (i[None, :] <= i[:, None])).astype(_BF16)


def _mixer_in_weights(w):
    q_end = 2 * GLA_DK + GLA_DV
    lr_end = q_end + GLA_GATE_RANK
    og_end = lr_end + GLA_DV
    cu_end = og_end + 2 * D_MODEL
    w_glr = jnp.pad(w[:, q_end:lr_end], ((0, 0), (0, RANK_PAD - GLA_GATE_RANK)))
    return jnp.concatenate([w[:, og_end:cu_end], w[:, :q_end], w[:, lr_end:og_end], w[:, cu_end:], w_glr],
                           axis=1).astype(_BF16)


def kernel(x, ffn1_pre_g, ffn1_w_in, ffn1_w_out, ffn1_post_g, mix_pre_g, w_mix_in, gla_w_gate_up, gla_b_gate,
           gla_norm_g, gla_w_proj, conv_dw_w, conv_dw_b, conv_ln_g, conv_ln_b, conv_w_proj, w_mix_out,
           mix_post_g, ffn2_pre_g, ffn2_w_in, ffn2_w_out, ffn2_post_g):
    batch, seq, d = x.shape
    depth = ffn1_pre_g.shape[0]
    tril = _chunk_tril(MIX_ROWS, GLA_CHUNK)
    for l in range(depth):
        row = lambda p: p[l].reshape(1, -1)
        x2d = _ffn(x.reshape(batch * seq, d), row(ffn1_pre_g), ffn1_w_in[l].astype(_BF16),
                   ffn1_w_out[l].astype(_BF16), row(ffn1_post_g))
        w_up = jnp.pad(gla_w_gate_up[l], ((0, RANK_PAD - GLA_GATE_RANK), (0, 0))).astype(_BF16)
        x1 = x2d.reshape(batch, seq, d)
        mixer_args = (row(mix_pre_g), _mixer_in_weights(w_mix_in[l]), tril, w_up, row(gla_b_gate),
                      row(gla_norm_g), gla_w_proj[l].astype(_BF16), conv_dw_w[l], row(conv_dw_b),
                      row(conv_ln_g), row(conv_ln_b), conv_w_proj[l].astype(_BF16), w_mix_out[l].astype(_BF16),
                      row(mix_post_g))
        ffn2_args = (row(ffn2_pre_g), ffn2_w_in[l].astype(_BF16), ffn2_w_out[l].astype(_BF16), row(ffn2_post_g))
        x2, min_decay = _mixer(False, x1, *mixer_args)
        x2d = lax.cond(
            jnp.min(min_decay) >= -GLA_FACTOR_LOG_RANGE,
            lambda x1, x2: _ffn(x2.reshape(batch * seq, d), *ffn2_args),
            lambda x1, x2: _ffn(_mixer(True, x1, *mixer_args).reshape(batch * seq, d), *ffn2_args),
            x1, x2)
        x = x2d.reshape(batch, seq, d)
    return x
```

```python
import functools
import itertools

import jax
import jax.numpy as jnp
from jax import lax
from jax.experimental import pallas as pl
from jax.experimental.pallas import tpu as pltpu

D_MODEL = 1024
D_FF = 2816
GLA_HEADS = 4
GLA_HK = 128
GLA_HV = 256
GLA_DK = GLA_HEADS * GLA_HK
GLA_DV = GLA_HEADS * GLA_HV
GLA_GATE_RANK = 16
GLA_GATE_NORMALIZER = 16.0
CONV_WIDTH = 31
EPS = 1e-6

SUBLANES = 8
LANES = 128
MXU_TILE = 256
VMEM_LIMIT_BYTES = 56 * 1024 * 1024

FFN_ROWS = 512
MIX_ROWS = 512
GLA_CHUNK = 128
ROW_BLOCK = 128
CONV_TAIL = 32
CONV_LANE_BLOCK = 128
RANK_PAD = LANES
GLA_FACTOR_LOG_RANGE = 60.0

_UA0 = 0
_UB0 = _UA0 + D_MODEL
_Z0 = _UB0 + D_MODEL
_Q0 = _Z0
_K0 = _Q0 + GLA_DK
_V0 = _K0 + GLA_DK
_OG0 = _V0 + GLA_DV
_GA0 = _OG0 + GLA_DV
_GB0 = _GA0 + D_MODEL
_GLR0 = _GB0 + D_MODEL
MIX_IN_PAD = _GLR0 + RANK_PAD

_F32 = jnp.float32
_BF16 = jnp.bfloat16


def _dot(a, b):
    return jnp.dot(a, b, preferred_element_type=_F32)


def _rms(x, g):
    return x * lax.rsqrt(jnp.mean(x * x, axis=-1, keepdims=True) + EPS) * g


def _row_mean_mxu(v):
    n = v.shape[-1]
    part = functools.reduce(jnp.add, [v[:, c:c + LANES] for c in range(0, n, LANES)])
    hi = part.astype(_BF16)
    lo = (part - hi.astype(_F32)).astype(_BF16)
    ones = jnp.ones((2 * LANES, LANES), _BF16)
    mean = _dot(jnp.concatenate([hi, lo], axis=1), ones) * (1.0 / n)
    return jnp.concatenate([mean] * (n // LANES), axis=1)


def _rms_mxu(x, g):
    return x * lax.rsqrt(_row_mean_mxu(x * x) + EPS) * g


def _interleave(*task_lists):
    longest = max(len(t) for t in task_lists)
    done = [0] * len(task_lists)
    for step in range(1, longest + 1):
        for i, tasks in enumerate(task_lists):
            upto = (step * len(tasks) + longest - 1) // longest
            for task in tasks[done[i]:upto]:
                task()
            done[i] = upto


def _ffn_kernel(x_ref, pre_g_ref, w_in_ref, w_out_ref, post_g_ref, o_ref):
    x = x_ref[...]
    r = lax.rsqrt(jnp.mean(x * x, axis=-1, keepdims=True) + EPS)
    xg = (x * pre_g_ref[...]).astype(_BF16)
    z = _dot(xg, w_in_ref[...]) * r
    acts = []
    for j in range(D_FF // MXU_TILE):
        gate = z[:, (2 * j) * MXU_TILE:(2 * j + 1) * MXU_TILE]
        up = z[:, (2 * j + 1) * MXU_TILE:(2 * j + 2) * MXU_TILE]
        acts.append((jax.nn.silu(gate) * up).astype(_BF16))
    act = jnp.concatenate(acts, axis=1)
    f = _dot(act, w_out_ref[...])
    o_ref[...] = x + _rms(f, 0.5 * post_g_ref[...])


def _resident(shape):
    return pl.BlockSpec(shape, lambda *_: (0,) * len(shape), pipeline_mode=pl.Buffered(1))


def _ffn(x2d, pre_g, w_in, w_out, post_g):
    n_tok = x2d.shape[0]
    assert n_tok % FFN_ROWS == 0
    row_spec = pl.BlockSpec((FFN_ROWS, D_MODEL), lambda i: (i, 0))
    return pl.pallas_call(
        _ffn_kernel,
        grid=(n_tok // FFN_ROWS,),
        in_specs=[row_spec, _resident((1, D_MODEL)), _resident((D_MODEL, 2 * D_FF)),
                  _resident((D_FF, D_MODEL)), _resident((1, D_MODEL))],
        out_specs=row_spec,
        out_shape=jax.ShapeDtypeStruct(x2d.shape, _F32),
        compiler_params=pltpu.CompilerParams(
            dimension_semantics=("arbitrary",), vmem_limit_bytes=VMEM_LIMIT_BYTES),
        name="ffn_half_step",
    )(x2d, pre_g, w_in, w_out, post_g)


def _mixer_kernel(robust, x_ref, pre_g_ref, w_in_ref, tril_ref, w_up_ref, b_gate_ref, norm_g_ref, w_gla_ref,
                  dw_w_ref, dw_b_ref, ln_g_ref, ln_b_ref, w_conv_ref, w_out_ref, post_g_ref, o_ref, *rest):
    if robust:
        state_ref, ubuf_ref, zbuf_ref, obuf_ref, ybuf_ref, cbuf_ref, ball_ref, attn_ref, h_ref, kbuf_ref = rest
    else:
        decay_ref, state_ref, ubuf_ref, zbuf_ref, obuf_ref, ybuf_ref, cbuf_ref = rest
    ts = MIX_ROWS
    n_chunks = ts // GLA_CHUNK
    row_blocks = [slice(r, r + ROW_BLOCK) for r in range(0, ts, ROW_BLOCK)]

    def zcols(c0, c1):
        return slice(c0 - _Z0, c1 - _Z0)

    @pl.when(pl.program_id(1) == 0)
    def _():
        state_ref[...] = jnp.zeros_like(state_ref)
        ubuf_ref[0:CONV_TAIL, :] = jnp.zeros((CONV_TAIL, D_MODEL), _F32)
        if robust:
            attn_ref[...] = jnp.zeros_like(attn_ref)

    h = _rms_mxu(x_ref[...], pre_g_ref[...]).astype(_BF16)

    def cumulative_log_decay():
        g_lr = _dot(h, w_in_ref[:, _GLR0:MIX_IN_PAD]).astype(_BF16)
        logit = _dot(g_lr, w_up_ref[...]) + b_gate_ref[...]
        log_a = jax.nn.log_sigmoid(logit) / GLA_GATE_NORMALIZER
        la_hi = log_a.astype(_BF16)
        la_lo = (log_a - la_hi.astype(_F32)).astype(_BF16)
        tril = tril_ref[...]
        return _dot(tril, la_hi) + _dot(tril, la_lo)

    if robust:
        h_ref[...] = h
        ball_ref[...] = cumulative_log_decay()
        decay_in_range = jnp.min(ball_ref[...]) >= -GLA_FACTOR_LOG_RANGE

        @pl.when(jnp.logical_not(decay_in_range))
        def _():
            col_id = lax.broadcasted_iota(jnp.int32, (GLA_CHUNK, GLA_CHUNK), 1)
            for hd in range(GLA_HEADS):
                ks = slice(hd * GLA_HK, (hd + 1) * GLA_HK)
                kbuf_ref[...] = _dot(h_ref[...], w_in_ref[:, _K0 + ks.start:_K0 + ks.stop])
                for c in range(n_chunks):
                    rows = slice(c * GLA_CHUNK, (c + 1) * GLA_CHUNK)
                    q_c = _dot(h_ref[rows, :], w_in_ref[:, _Q0 + ks.start:_Q0 + ks.stop]) * (GLA_HK ** -0.5)
                    b_c = ball_ref[rows, ks]

                    def columns(jb, acc, rows=rows, ks=ks, q_c=q_c, b_c=b_c):
                        src = pl.ds(pl.multiple_of(rows.start + jb * SUBLANES, SUBLANES), SUBLANES)
                        b_j, k_j = ball_ref[src, ks], kbuf_ref[src, :]
                        for r in range(SUBLANES):
                            w = jnp.exp(jnp.minimum(b_c - b_j[r:r + 1], 0.0))
                            a_j = jnp.sum(q_c * k_j[r:r + 1] * w, axis=-1, keepdims=True)
                            acc = acc + jnp.where(col_id == jb * SUBLANES + r, a_j, 0.0)
                        return acc

                    attn_ref[c, hd] = lax.fori_loop(0, GLA_CHUNK // SUBLANES, columns,
                                                    jnp.zeros((GLA_CHUNK, GLA_CHUNK), _F32))

        h = h_ref[...]
        b_all = ball_ref[...]

    ua = _dot(h, w_in_ref[:, _UA0:_UB0])
    ub = _dot(h, w_in_ref[:, _UB0:_Z0])
    ubuf_ref[CONV_TAIL:CONV_TAIL + ts, :] = ua * jax.nn.sigmoid(ub)

    tap0 = CONV_TAIL - (CONV_WIDTH - 1)
    win = ROW_BLOCK + CONV_TAIL

    def conv_block(rows, cs):
        xw = ubuf_ref[rows.start:rows.start + win, cs]
        acc = jnp.broadcast_to(dw_b_ref[:, cs], (ROW_BLOCK, CONV_LANE_BLOCK))
        for r in range(SUBLANES):
            xr = xw if r == 0 else pltpu.roll(xw, win - r, axis=0)
            for a in range(win // SUBLANES):
                tau = SUBLANES * a + r - tap0
                if 0 <= tau < CONV_WIDTH:
                    acc = acc + xr[SUBLANES * a:SUBLANES * a + ROW_BLOCK] * dw_w_ref[tau:tau + 1, cs]
        ybuf_ref[rows, cs] = acc

    def proj_piece(c0):
        zbuf_ref[:, zcols(c0, c0 + MXU_TILE)] = _dot(h, w_in_ref[:, c0:c0 + MXU_TILE])

    conv_tasks = [lambda rows=rows, c=c: conv_block(rows, slice(c, c + CONV_LANE_BLOCK))
                  for rows in row_blocks for c in range(0, D_MODEL, CONV_LANE_BLOCK)]
    proj_tasks = [lambda c=c: proj_piece(c) for c in range(_Z0, _GLR0, MXU_TILE)]
    _interleave(conv_tasks, proj_tasks)
    ubuf_ref[0:CONV_TAIL, :] = ubuf_ref[ts:ts + CONV_TAIL, :]

    if not robust:
        b_all = cumulative_log_decay()
        tile_min = jnp.min(b_all, axis=0, keepdims=True)
        tile_min = functools.reduce(jnp.minimum, [tile_min[:, c:c + LANES] for c in range(0, GLA_DK, LANES)])
        decay_ref[...] = jnp.broadcast_to(tile_min, (SUBLANES, LANES))

    def chunk_slices(c, hd):
        rows = slice(c * GLA_CHUNK, (c + 1) * GLA_CHUNK)
        ks = slice(hd * GLA_HK, (hd + 1) * GLA_HK)
        vs = slice(hd * GLA_HV, (hd + 1) * GLA_HV)
        return (rows, ks, vs, zcols(_Q0 + ks.start, _Q0 + ks.stop), zcols(_K0 + ks.start, _K0 + ks.stop),
                zcols(_V0 + vs.start, _V0 + vs.stop))

    row = lax.broadcasted_iota(jnp.int32, (GLA_CHUNK, GLA_CHUNK), 0)
    col = lax.broadcasted_iota(jnp.int32, (GLA_CHUNK, GLA_CHUNK), 1)
    causal = col <= row
    mid = GLA_CHUNK // 2

    def gla_chunk_head(c, hd):
        rows, ks, vs, qc, kc, vc = chunk_slices(c, hd)
        b = b_all[rows, ks]
        b_mid = b[mid - 1:mid]
        b_last = b[GLA_CHUNK - 1:GLA_CHUNK]
        q_c = zbuf_ref[rows, qc] * (GLA_HK ** -0.5)
        k_c = zbuf_ref[rows, kc]
        v_h = zbuf_ref[rows, vc].astype(_BF16)
        e = b - b_mid
        qd = (q_c * jnp.exp(e)).astype(_BF16)
        kd = (k_c * jnp.exp(-e)).astype(_BF16)
        qb = (q_c * jnp.exp(b)).astype(_BF16)
        kl = (k_c * jnp.exp(b_last - b)).astype(_BF16)
        attn = lax.dot_general(qd, kd, (((1,), (1,)), ((), ())), preferred_element_type=_F32)
        if robust:
            attn = jnp.where(decay_in_range, attn, attn_ref[c, hd])
        attn = jnp.where(causal, attn, 0.0).astype(_BF16)
        s_h = state_ref[hd]
        obuf_ref[rows, vs] = _dot(attn, v_h) + _dot(qb, s_h.astype(_BF16))
        a_col = jnp.broadcast_to(jnp.exp(b_last), (GLA_HK, GLA_HK)).T
        a_col = jnp.concatenate([a_col] * (GLA_HV // GLA_HK), axis=1)
        state_ref[hd] = s_h * a_col + lax.dot_general(
            kl, v_h, (((0,), (0,)), ((), ())), preferred_element_type=_F32)

    def ln_silu(rows):
        yc = ybuf_ref[rows, :]
        yc0 = yc - _row_mean_mxu(yc)
        var = _row_mean_mxu(yc0 * yc0)
        ln = yc0 * lax.rsqrt(var + EPS) * ln_g_ref[...] + ln_b_ref[...]
        cbuf_ref[rows, :] = jax.nn.silu(ln).astype(_BF16)

    def in_place(fn, rows, c0):
        cols = zcols(c0, c0 + D_MODEL)
        zbuf_ref[rows, cols] = fn(zbuf_ref[rows, cols])

    gla_tasks = [lambda c=c, hd=hd: gla_chunk_head(c, hd) for c in range(n_chunks) for hd in range(GLA_HEADS)]
    vpu_tasks = list(itertools.chain.from_iterable(
        (lambda rows=rows: ln_silu(rows),
         lambda rows=rows: in_place(jax.nn.silu, rows, _OG0),
         lambda rows=rows: in_place(jax.nn.sigmoid, rows, _GA0),
         lambda rows=rows: in_place(jax.nn.sigmoid, rows, _GB0)) for rows in row_blocks))
    _interleave(gla_tasks, vpu_tasks)

    y_conv = _dot(cbuf_ref[...], w_conv_ref[...])
    norm_g = norm_g_ref[...]
    heads = []
    for hd in range(GLA_HEADS):
        vs = slice(hd * GLA_HV, (hd + 1) * GLA_HV)
        sil_og = zbuf_ref[:, zcols(_OG0 + vs.start, _OG0 + vs.stop)]
        heads.append((_rms_mxu(obuf_ref[:, vs], norm_g) * sil_og).astype(_BF16))
    y_gla = _dot(jnp.concatenate(heads, axis=1), w_gla_ref[...])

    gate_a = zbuf_ref[:, zcols(_GA0, _GB0)]
    gate_b = zbuf_ref[:, zcols(_GB0, _GLR0)]
    m = _dot((gate_a * y_gla + gate_b * y_conv).astype(_BF16), w_out_ref[...])
    o_ref[...] = x_ref[...] + _rms_mxu(m, post_g_ref[...])


def _mixer(robust, x, pre_g, w_in, tril, w_up, b_gate, norm_g, w_gla, dw_w, dw_b, ln_g, ln_b, w_conv, w_out,
           post_g):
    batch, seq, _ = x.shape
    ts = MIX_ROWS
    assert seq % ts == 0 and ts % GLA_CHUNK == 0 and ts % ROW_BLOCK == 0
    n_seq_tiles = seq // ts
    tok_spec = pl.BlockSpec((None, ts, D_MODEL), lambda b, s: (b, s, 0))
    consts = (pre_g, w_in, tril, w_up, b_gate, norm_g, w_gla, dw_w, dw_b, ln_g, ln_b, w_conv, w_out, post_g)
    out_specs, out_shape = tok_spec, jax.ShapeDtypeStruct(x.shape, _F32)
    scratch = [
        pltpu.VMEM((GLA_HEADS, GLA_HK, GLA_HV), _F32),
        pltpu.VMEM((CONV_TAIL + ts, D_MODEL), _F32),
        pltpu.VMEM((ts, _GLR0 - _Z0), _F32),
        pltpu.VMEM((ts, GLA_DV), _F32),
        pltpu.VMEM((ts, D_MODEL), _F32),
        pltpu.VMEM((ts, D_MODEL), _BF16),
    ]
    if robust:
        scratch += [
            pltpu.VMEM((ts, GLA_DK), _F32),
            pltpu.VMEM((ts // GLA_CHUNK, GLA_HEADS, GLA_CHUNK, GLA_CHUNK), _F32),
            pltpu.VMEM((ts, D_MODEL), _BF16),
            pltpu.VMEM((ts, GLA_HK), _F32),
        ]
    else:
        out_specs = [tok_spec, pl.BlockSpec((None, None, SUBLANES, LANES), lambda b, s: (b, s, 0, 0))]
        out_shape = [out_shape, jax.ShapeDtypeStruct((batch, n_seq_tiles, SUBLANES, LANES), _F32)]
    return pl.pallas_call(
        functools.partial(_mixer_kernel, robust),
        grid=(batch, n_seq_tiles),
        in_specs=[tok_spec] + [_resident(c.shape) for c in consts],
        out_specs=out_specs,
        out_shape=out_shape,
        scratch_shapes=scratch,
        compiler_params=pltpu.CompilerParams(
            dimension_semantics=("arbitrary", "arbitrary"), vmem_limit_bytes=VMEM_LIMIT_BYTES),
        name="gla_conv_mixer_checked" if robust else "gla_conv_mixer",
    )(x, *consts)


def _ffn_in_weights(w):
    d = w.shape[0]
    gate = w[:, :D_FF].reshape(d, D_FF // MXU_TILE, MXU_TILE)
    up = w[:, D_FF:].reshape(d, D_FF // MXU_TILE, MXU_TILE)
    return jnp.stack([gate, up], axis=2).reshape(d, 2 * D_FF).astype(_BF16)


def _chunk_tril(n, chunk):
    i = jnp.arange(n)
    same = (i[:, None] // chunk) == (i[None, :] // chunk)
    return (same & (i[None, :] <= i[:, None])).astype(_BF16)


def _mixer_in_weights(w):
    q_end = 2 * GLA_DK + GLA_DV
    lr_end = q_end + GLA_GATE_RANK
    og_end = lr_end + GLA_DV
    cu_end = og_end + 2 * D_MODEL
    w_glr = jnp.pad(w[:, q_end:lr_end], ((0, 0), (0, RANK_PAD - GLA_GATE_RANK)))
    return jnp.concatenate([w[:, og_end:cu_end], w[:, :q_end], w[:, lr_end:og_end], w[:, cu_end:], w_glr],
                           axis=1).astype(_BF16)


def kernel(x, ffn1_pre_g, ffn1_w_in, ffn1_w_out, ffn1_post_g, mix_pre_g, w_mix_in, gla_w_gate_up, gla_b_gate,
           gla_norm_g, gla_w_proj, conv_dw_w, conv_dw_b, conv_ln_g, conv_ln_b, conv_w_proj, w_mix_out,
           mix_post_g, ffn2_pre_g, ffn2_w_in, ffn2_w_out, ffn2_post_g):
    batch, seq, d = x.shape
    depth = ffn1_pre_g.shape[0]
    tril = _chunk_tril(MIX_ROWS, GLA_CHUNK)
    for l in range(depth):
        row = lambda p: p[l].reshape(1, -1)
        x2d = _ffn(x.reshape(batch * seq, d), row(ffn1_pre_g), _ffn_in_weights(ffn1_w_in[l]),
                   ffn1_w_out[l].astype(_BF16), row(ffn1_post_g))
        w_up = jnp.pad(gla_w_gate_up[l], ((0, RANK_PAD - GLA_GATE_RANK), (0, 0))).astype(_BF16)
        x1 = x2d.reshape(batch, seq, d)
        mixer_args = (row(mix_pre_g), _mixer_in_weights(w_mix_in[l]), tril, w_up, row(gla_b_gate),
                      row(gla_norm_g), gla_w_proj[l].astype(_BF16), conv_dw_w[l], row(conv_dw_b),
                      row(conv_ln_g), row(conv_ln_b), conv_w_proj[l].astype(_BF16), w_mix_out[l].astype(_BF16),
                      row(mix_post_g))
        ffn2_args = (row(ffn2_pre_g), _ffn_in_weights(ffn2_w_in[l]), ffn2_w_out[l].astype(_BF16), row(ffn2_post_g))
        x2, min_decay = _mixer(False, x1, *mixer_args)
        x2d = lax.cond(
            jnp.min(min_decay) >= -GLA_FACTOR_LOG_RANGE,
            lambda x1, x2: _ffn(x2.reshape(batch * seq, d), *ffn2_args),
            lambda x1, x2: _ffn(_mixer(True, x1, *mixer_args).reshape(batch * seq, d), *ffn2_args),
            x1, x2)
        x = x2d.reshape(batch, seq, d)
    return x
```

```python
import functools
import itertools

import jax
import jax.numpy as jnp
from jax import lax
from jax.experimental import pallas as pl
from jax.experimental.pallas import tpu as pltpu

D_MODEL = 1024
D_FF = 2816
GLA_HEADS = 4
GLA_HK = 128
GLA_HV = 256
GLA_DK = GLA_HEADS * GLA_HK
GLA_DV = GLA_HEADS * GLA_HV
GLA_GATE_RANK = 16
GLA_GATE_NORMALIZER = 16.0
CONV_WIDTH = 31
EPS = 1e-6

SUBLANES = 8
LANES = 128
MXU_TILE = 256
VMEM_LIMIT_BYTES = 56 * 1024 * 1024

FFN_ROWS = 512
MIX_ROWS = 512
GLA_CHUNK = 128
ROW_BLOCK = 128
CONV_TAIL = 32
CONV_LANE_BLOCK = 128
RANK_PAD = LANES
GLA_FACTOR_LOG_RANGE = 60.0

_UA0 = 0
_UB0 = _UA0 + D_MODEL
_Z0 = _UB0 + D_MODEL
_Q0 = _Z0
_K0 = _Q0 + GLA_DK
_V0 = _K0 + GLA_DK
_OG0 = _V0 + GLA_DV
_GA0 = _OG0 + GLA_DV
_GB0 = _GA0 + D_MODEL
_GLR0 = _GB0 + D_MODEL
MIX_IN_PAD = _GLR0 + RANK_PAD

_F32 = jnp.float32
_BF16 = jnp.bfloat16


def _dot(a, b):
    return jnp.dot(a, b, preferred_element_type=_F32)


def _rms(x, g):
    return x * lax.rsqrt(jnp.mean(x * x, axis=-1, keepdims=True) + EPS) * g


def _row_mean_mxu(v):
    n = v.shape[-1]
    part = functools.reduce(jnp.add, [v[:, c:c + LANES] for c in range(0, n, LANES)])
    hi = part.astype(_BF16)
    lo = (part - hi.astype(_F32)).astype(_BF16)
    ones = jnp.ones((2 * LANES, LANES), _BF16)
    mean = _dot(jnp.concatenate([hi, lo], axis=1), ones) * (1.0 / n)
    return jnp.concatenate([mean] * (n // LANES), axis=1)


def _rms_mxu(x, g):
    return x * lax.rsqrt(_row_mean_mxu(x * x) + EPS) * g


def _interleave(*task_lists):
    longest = max(len(t) for t in task_lists)
    done = [0] * len(task_lists)
    for step in range(1, longest + 1):
        for i, tasks in enumerate(task_lists):
            upto = (step * len(tasks) + longest - 1) // longest
            for task in tasks[done[i]:upto]:
                task()
            done[i] = upto


def _ffn_kernel(x_ref, pre_g_ref, w_in_ref, w_out_ref, post_g_ref, o_ref):
    x = x_ref[...]
    r = lax.rsqrt(jnp.mean(x * x, axis=-1, keepdims=True) + EPS)
    xg = (x * pre_g_ref[...]).astype(_BF16)
    acts = []
    for c0 in range(0, D_FF, MXU_TILE):
        gate = _dot(xg, w_in_ref[:, c0:c0 + MXU_TILE]) * r
        up = _dot(xg, w_in_ref[:, D_FF + c0:D_FF + c0 + MXU_TILE]) * r
        acts.append((jax.nn.silu(gate) * up).astype(_BF16))
    act = jnp.concatenate(acts, axis=1)
    f = _dot(act, w_out_ref[...])
    o_ref[...] = x + _rms(f, 0.5 * post_g_ref[...])


def _resident(shape):
    return pl.BlockSpec(shape, lambda *_: (0,) * len(shape), pipeline_mode=pl.Buffered(1))


def _ffn(x2d, pre_g, w_in, w_out, post_g):
    n_tok = x2d.shape[0]
    assert n_tok % FFN_ROWS == 0
    row_spec = pl.BlockSpec((FFN_ROWS, D_MODEL), lambda i: (i, 0))
    return pl.pallas_call(
        _ffn_kernel,
        grid=(n_tok // FFN_ROWS,),
        in_specs=[row_spec, _resident((1, D_MODEL)), _resident((D_MODEL, 2 * D_FF)),
                  _resident((D_FF, D_MODEL)), _resident((1, D_MODEL))],
        out_specs=row_spec,
        out_shape=jax.ShapeDtypeStruct(x2d.shape, _F32),
        compiler_params=pltpu.CompilerParams(
            dimension_semantics=("arbitrary",), vmem_limit_bytes=VMEM_LIMIT_BYTES),
        name="ffn_half_step",
    )(x2d, pre_g, w_in, w_out, post_g)


def _mixer_kernel(robust, x_ref, pre_g_ref, w_in_ref, tril_ref, w_up_ref, b_gate_ref, norm_g_ref, w_gla_ref,
                  dw_w_ref, dw_b_ref, ln_g_ref, ln_b_ref, w_conv_ref, w_out_ref, post_g_ref, o_ref, *rest):
    if robust:
        state_ref, ubuf_ref, zbuf_ref, obuf_ref, ybuf_ref, cbuf_ref, ball_ref, attn_ref, h_ref, kbuf_ref = rest
    else:
        decay_ref, state_ref, ubuf_ref, zbuf_ref, obuf_ref, ybuf_ref, cbuf_ref = rest
    ts = MIX_ROWS
    n_chunks = ts // GLA_CHUNK
    row_blocks = [slice(r, r + ROW_BLOCK) for r in range(0, ts, ROW_BLOCK)]

    def zcols(c0, c1):
        return slice(c0 - _Z0, c1 - _Z0)

    @pl.when(pl.program_id(1) == 0)
    def _():
        state_ref[...] = jnp.zeros_like(state_ref)
        ubuf_ref[0:CONV_TAIL, :] = jnp.zeros((CONV_TAIL, D_MODEL), _F32)
        if robust:
            attn_ref[...] = jnp.zeros_like(attn_ref)

    h = _rms_mxu(x_ref[...], pre_g_ref[...]).astype(_BF16)

    def cumulative_log_decay():
        g_lr = _dot(h, w_in_ref[:, _GLR0:MIX_IN_PAD]).astype(_BF16)
        logit = _dot(g_lr, w_up_ref[...]) + b_gate_ref[...]
        log_a = jax.nn.log_sigmoid(logit) / GLA_GATE_NORMALIZER
        la_hi = log_a.astype(_BF16)
        la_lo = (log_a - la_hi.astype(_F32)).astype(_BF16)
        tril = tril_ref[...]
        return _dot(tril, la_hi) + _dot(tril, la_lo)

    if robust:
        h_ref[...] = h
        ball_ref[...] = cumulative_log_decay()
        decay_in_range = jnp.min(ball_ref[...]) >= -GLA_FACTOR_LOG_RANGE

        @pl.when(jnp.logical_not(decay_in_range))
        def _():
            col_id = lax.broadcasted_iota(jnp.int32, (GLA_CHUNK, GLA_CHUNK), 1)
            for hd in range(GLA_HEADS):
                ks = slice(hd * GLA_HK, (hd + 1) * GLA_HK)
                kbuf_ref[...] = _dot(h_ref[...], w_in_ref[:, _K0 + ks.start:_K0 + ks.stop])
                for c in range(n_chunks):
                    rows = slice(c * GLA_CHUNK, (c + 1) * GLA_CHUNK)
                    q_c = _dot(h_ref[rows, :], w_in_ref[:, _Q0 + ks.start:_Q0 + ks.stop]) * (GLA_HK ** -0.5)
                    b_c = ball_ref[rows, ks]

                    def columns(jb, acc, rows=rows, ks=ks, q_c=q_c, b_c=b_c):
                        src = pl.ds(pl.multiple_of(rows.start + jb * SUBLANES, SUBLANES), SUBLANES)
                        b_j, k_j = ball_ref[src, ks], kbuf_ref[src, :]
                        for r in range(SUBLANES):
                            w = jnp.exp(jnp.minimum(b_c - b_j[r:r + 1], 0.0))
                            a_j = jnp.sum(q_c * k_j[r:r + 1] * w, axis=-1, keepdims=True)
                            acc = acc + jnp.where(col_id == jb * SUBLANES + r, a_j, 0.0)
                        return acc

                    attn_ref[c, hd] = lax.fori_loop(0, GLA_CHUNK // SUBLANES, columns,
                                                    jnp.zeros((GLA_CHUNK, GLA_CHUNK), _F32))

        h = h_ref[...]
        b_all = ball_ref[...]

    ua = _dot(h, w_in_ref[:, _UA0:_UB0])
    ub = _dot(h, w_in_ref[:, _UB0:_Z0])
    ubuf_ref[CONV_TAIL:CONV_TAIL + ts, :] = ua * jax.nn.sigmoid(ub)

    tap0 = CONV_TAIL - (CONV_WIDTH - 1)
    win = ROW_BLOCK + CONV_TAIL

    def conv_block(rows, cs):
        xw = ubuf_ref[rows.start:rows.start + win, cs]
        acc = jnp.broadcast_to(dw_b_ref[:, cs], (ROW_BLOCK, CONV_LANE_BLOCK))
        for r in range(SUBLANES):
            xr = xw if r == 0 else pltpu.roll(xw, win - r, axis=0)
            for a in range(win // SUBLANES):
                tau = SUBLANES * a + r - tap0
                if 0 <= tau < CONV_WIDTH:
                    acc = acc + xr[SUBLANES * a:SUBLANES * a + ROW_BLOCK] * dw_w_ref[tau:tau + 1, cs]
        ybuf_ref[rows, cs] = acc

    def proj_piece(c0):
        zbuf_ref[:, zcols(c0, c0 + MXU_TILE)] = _dot(h, w_in_ref[:, c0:c0 + MXU_TILE])

    conv_tasks = [lambda rows=rows, c=c: conv_block(rows, slice(c, c + CONV_LANE_BLOCK))
                  for rows in row_blocks for c in range(0, D_MODEL, CONV_LANE_BLOCK)]
    proj_tasks = [lambda c=c: proj_piece(c) for c in range(_Z0, _GLR0, MXU_TILE)]
    _interleave(conv_tasks, proj_tasks)
    ubuf_ref[0:CONV_TAIL, :] = ubuf_ref[ts:ts + CONV_TAIL, :]

    if not robust:
        b_all = cumulative_log_decay()
        tile_min = jnp.min(b_all, axis=0, keepdims=True)
        tile_min = functools.reduce(jnp.minimum, [tile_min[:, c:c + LANES] for c in range(0, GLA_DK, LANES)])
        decay_ref[...] = jnp.broadcast_to(tile_min, (SUBLANES, LANES))

    def chunk_slices(c, hd):
        rows = slice(c * GLA_CHUNK, (c + 1) * GLA_CHUNK)
        ks = slice(hd * GLA_HK, (hd + 1) * GLA_HK)
        vs = slice(hd * GLA_HV, (hd + 1) * GLA_HV)
        return (rows, ks, vs, zcols(_Q0 + ks.start, _Q0 + ks.stop), zcols(_K0 + ks.start, _K0 + ks.stop),
                zcols(_V0 + vs.start, _V0 + vs.stop))

    row = lax.broadcasted_iota(jnp.int32, (GLA_CHUNK, GLA_CHUNK), 0)
    col = lax.broadcasted_iota(jnp.int32, (GLA_CHUNK, GLA_CHUNK), 1)
    causal = col <= row
    mid = GLA_CHUNK // 2

    def gla_chunk_head(c, hd):
        rows, ks, vs, qc, kc, vc = chunk_slices(c, hd)
        b = b_all[rows, ks]
        b_mid = b[mid - 1:mid]
        b_last = b[GLA_CHUNK - 1:GLA_CHUNK]
        q_c = zbuf_ref[rows, qc] * (GLA_HK ** -0.5)
        k_c = zbuf_ref[rows, kc]
        v_h = zbuf_ref[rows, vc].astype(_BF16)
        e = b - b_mid
        qd = (q_c * jnp.exp(e)).astype(_BF16)
        kd = (k_c * jnp.exp(-e)).astype(_BF16)
        qb = (q_c * jnp.exp(b)).astype(_BF16)
        kl = (k_c * jnp.exp(b_last - b)).astype(_BF16)
        attn = lax.dot_general(qd, kd, (((1,), (1,)), ((), ())), preferred_element_type=_F32)
        if robust:
            attn = jnp.where(decay_in_range, attn, attn_ref[c, hd])
        attn = jnp.where(causal, attn, 0.0).astype(_BF16)
        s_h = state_ref[hd]
        obuf_ref[rows, vs] = _dot(attn, v_h) + _dot(qb, s_h.astype(_BF16))
        a_col = jnp.broadcast_to(jnp.exp(b_last), (GLA_HK, GLA_HK)).T
        a_col = jnp.concatenate([a_col] * (GLA_HV // GLA_HK), axis=1)
        state_ref[hd] = s_h * a_col + lax.dot_general(
            kl, v_h, (((0,), (0,)), ((), ())), preferred_element_type=_F32)

    def ln_silu(rows):
        yc = ybuf_ref[rows, :]
        yc0 = yc - _row_mean_mxu(yc)
        var = _row_mean_mxu(yc0 * yc0)
        ln = yc0 * lax.rsqrt(var + EPS) * ln_g_ref[...] + ln_b_ref[...]
        cbuf_ref[rows, :] = jax.nn.silu(ln).astype(_BF16)

    def in_place(fn, rows, c0):
        cols = zcols(c0, c0 + D_MODEL)
        zbuf_ref[rows, cols] = fn(zbuf_ref[rows, cols])

    gla_tasks = [lambda c=c, hd=hd: gla_chunk_head(c, hd) for c in range(n_chunks) for hd in range(GLA_HEADS)]
    vpu_tasks = list(itertools.chain.from_iterable(
        (lambda rows=rows: ln_silu(rows),
         lambda rows=rows: in_place(jax.nn.silu, rows, _OG0),
         lambda rows=rows: in_place(jax.nn.sigmoid, rows, _GA0),
         lambda rows=rows: in_place(jax.nn.sigmoid, rows, _GB0)) for rows in row_blocks))
    _interleave(gla_tasks, vpu_tasks)

    y_conv = _dot(cbuf_ref[...], w_conv_ref[...])
    norm_g = norm_g_ref[...]
    heads = []
    for hd in range(GLA_HEADS):
        vs = slice(hd * GLA_HV, (hd + 1) * GLA_HV)
        sil_og = zbuf_ref[:, zcols(_OG0 + vs.start, _OG0 + vs.stop)]
        heads.append((_rms_mxu(obuf_ref[:, vs], norm_g) * sil_og).astype(_BF16))
    y_gla = _dot(jnp.concatenate(heads, axis=1), w_gla_ref[...])

    gate_a = zbuf_ref[:, zcols(_GA0, _GB0)]
    gate_b = zbuf_ref[:, zcols(_GB0, _GLR0)]
    m = _dot((gate_a * y_gla + gate_b * y_conv).astype(_BF16), w_out_ref[...])
    o_ref[...] = x_ref[...] + _rms_mxu(m, post_g_ref[...])


def _mixer(robust, x, pre_g, w_in, tril, w_up, b_gate, norm_g, w_gla, dw_w, dw_b, ln_g, ln_b, w_conv, w_out,
           post_g):
    batch, seq, _ = x.shape
    ts = MIX_ROWS
    assert seq % ts == 0 and ts % GLA_CHUNK == 0 and ts % ROW_BLOCK == 0
    n_seq_tiles = seq // ts
    tok_spec = pl.BlockSpec((None, ts, D_MODEL), lambda b, s: (b, s, 0))
    consts = (pre_g, w_in, tril, w_up, b_gate, norm_g, w_gla, dw_w, dw_b, ln_g, ln_b, w_conv, w_out, post_g)
    out_specs, out_shape = tok_spec, jax.ShapeDtypeStruct(x.shape, _F32)
    scratch = [
        pltpu.VMEM((GLA_HEADS, GLA_HK, GLA_HV), _F32),
        pltpu.VMEM((CONV_TAIL + ts, D_MODEL), _F32),
        pltpu.VMEM((ts, _GLR0 - _Z0), _F32),
        pltpu.VMEM((ts, GLA_DV), _F32),
        pltpu.VMEM((ts, D_MODEL), _F32),
        pltpu.VMEM((ts, D_MODEL), _BF16),
    ]
    if robust:
        scratch += [
            pltpu.VMEM((ts, GLA_DK), _F32),
            pltpu.VMEM((ts // GLA_CHUNK, GLA_HEADS, GLA_CHUNK, GLA_CHUNK), _F32),
            pltpu.VMEM((ts, D_MODEL), _BF16),
            pltpu.VMEM((ts, GLA_HK), _F32),
        ]
    else:
        out_specs = [tok_spec, pl.BlockSpec((None, None, SUBLANES, LANES), lambda b, s: (b, s, 0, 0))]
        out_shape = [out_shape, jax.ShapeDtypeStruct((batch, n_seq_tiles, SUBLANES, LANES), _F32)]
    return pl.pallas_call(
        functools.partial(_mixer_kernel, robust),
        grid=(batch, n_seq_tiles),
        in_specs=[tok_spec] + [_resident(c.shape) for c in consts],
        out_specs=out_specs,
        out_shape=out_shape,
        scratch_shapes=scratch,
        compiler_params=pltpu.CompilerParams(
            dimension_semantics=("arbitrary", "arbitrary"), vmem_limit_bytes=VMEM_LIMIT_BYTES),
        name="gla_conv_mixer_checked" if robust else "gla_conv_mixer",
    )(x, *consts)


def _chunk_tril(n, chunk):
    i = jnp.arange(n)
    same = (i[:, None] // chunk) == (i[None, :] // chunk)
    return (same & (i[None, :] <= i[:, None])).astype(_BF16)


def _mixer_in_weights(w):
    q_end = 2 * GLA_DK + GLA_DV
    lr_end = q_end + GLA_GATE_RANK
    og_end = lr_end + GLA_DV
    cu_end = og_end + 2 * D_MODEL
    w_glr = jnp.pad(w[:, q_end:lr_end], ((0, 0), (0, RANK_PAD - GLA_GATE_RANK)))
    return jnp.concatenate([w[:, og_end:cu_end], w[:, :q_end], w[:, lr_end:og_end], w[:, cu_end:], w_glr],
                           axis=1).astype(_BF16)


def kernel(x, ffn1_pre_g, ffn1_w_in, ffn1_w_out, ffn1_post_g, mix_pre_g, w_mix_in, gla_w_gate_up, gla_b_gate,
           gla_norm_g, gla_w_proj, conv_dw_w, conv_dw_b, conv_ln_g, conv_ln_b, conv_w_proj, w_mix_out,
           mix_post_g, ffn2_pre_g, ffn2_w_in, ffn2_w_out, ffn2_post_g):
    batch, seq, d = x.shape
    depth = ffn1_pre_g.shape[0]
    tril = _chunk_tril(MIX_ROWS, GLA_CHUNK)
    for l in range(depth):
        row = lambda p: p[l].reshape(1, -1)
        x2d = _ffn(x.reshape(batch * seq, d), row(ffn1_pre_g), ffn1_w_in[l].astype(_BF16),
                   ffn1_w_out[l].astype(_BF16), row(ffn1_post_g))
        w_up = jnp.pad(gla_w_gate_up[l], ((0, RANK_PAD - GLA_GATE_RANK), (0, 0))).astype(_BF16)
        x1 = x2d.reshape(batch, seq, d)
        mixer_args = (row(mix_pre_g), _mixer_in_weights(w_mix_in[l]), tril, w_up, row(gla_b_gate),
                      row(gla_norm_g), gla_w_proj[l].astype(_BF16), conv_dw_w[l], row(conv_dw_b),
                      row(conv_ln_g), row(conv_ln_b), conv_w_proj[l].astype(_BF16), w_mix_out[l].astype(_BF16),
                      row(mix_post_g))
        ffn2_args = (row(ffn2_pre_g), ffn2_w_in[l].astype(_BF16), ffn2_w_out[l].astype(_BF16), row(ffn2_post_g))
        x2, min_decay = _mixer(False, x1, *mixer_args)
        x2d = lax.cond(
            jnp.min(min_decay) >= -GLA_FACTOR_LOG_RANGE,
            lambda x1, x2: _ffn(x2.reshape(batch * seq, d), *ffn2_args),
            lambda x1, x2: _ffn(_mixer(True, x1, *mixer_args).reshape(batch * seq, d), *ffn2_args),
            x1, x2)
        x = x2d.reshape(batch, seq, d)
    return x
```
